```python
import jax, jax.numpy as jnp
from jax import lax
import numpy as np

D_MODEL = 1024
BATCH = 8
SEQ = 4096
DEPTH = 2

PLE_DIM = 256
MIX_WIDTH = D_MODEL
POOL_WIDTH = MIX_WIDTH // 2
POOL_WINDOWS = (2, 4, 8, 16)
N_POOL_GROUPS = len(POOL_WINDOWS)
POOL_GROUP_DIM = POOL_WIDTH // N_POOL_GROUPS
SB_WIDTH = MIX_WIDTH - POOL_WIDTH
SB_HEAD_DIM = 64
SB_HEADS = SB_WIDTH // SB_HEAD_DIM
SB_BLOCK = 128
GDN_HEAD_DIM = 128
GDN_HEADS = MIX_WIDTH // GDN_HEAD_DIM
GDN_CONV = 4
GDN_CHUNK = 64
FFN_DIM = 2816
FFN_CONV = 3
EPS = 1e-6
N_EVEN = (DEPTH + 1) // 2
N_ODD = DEPTH // 2
EVEN_IN = POOL_WIDTH + 3 * SB_WIDTH
ODD_IN = 4 * MIX_WIDTH + 2 * GDN_HEADS

kernel_name = 'hybrid_pool_stickbreak_gdn_convffn_ple'


def rmsnorm(x, gain):
    xf = x.astype(jnp.float32)
    y = xf * lax.rsqrt(jnp.mean(xf * xf, axis=-1, keepdims=True) + EPS)
    return (y * gain.astype(jnp.float32)).astype(x.dtype)


def l2norm(x):
    return x * lax.rsqrt(jnp.sum(x * x, axis=-1, keepdims=True) + EPS)


def causal_dwconv(x, w):
    K = w.shape[0]
    T = x.shape[1]
    xp = jnp.pad(x, ((0, 0), (K - 1, 0), (0, 0)))
    return sum(xp[:, i:i + T] * w[i] for i in range(K))


def pool_mixer(u, pool_w, pool_scale):
    B, T, _ = u.shape
    ug = u.reshape(B, T, N_POOL_GROUPS, POOL_GROUP_DIM).astype(jnp.float32)
    cs = jnp.pad(jnp.cumsum(ug, axis=1), ((0, 0), (1, 0), (0, 0), (0, 0)))
    t = jnp.arange(T)
    win = jnp.array(POOL_WINDOWS, dtype=jnp.int32)
    start = jnp.maximum(t[:, None] + 1 - win[None, :], 0)
    g_idx = jnp.arange(N_POOL_GROUPS)[None, :]
    window_sum = cs[:, 1:] - cs[:, start, g_idx]
    count = (t[:, None] + 1 - start).astype(jnp.float32)
    y = window_sum / count[None, :, :, None] - ug
    y = jnp.einsum('btgc,gcd->btgd', y, pool_w.astype(jnp.float32))
    return (y.reshape(B, T, POOL_WIDTH) * pool_scale.astype(jnp.float32)).astype(u.dtype)


def stick_breaking_attention(q, k, v):
    T = q.shape[2]
    scale = SB_HEAD_DIM ** -0.5
    vf = v.astype(jnp.float32)
    outs = []
    for blk in range(T // SB_BLOCK):
        q0 = blk * SB_BLOCK
        end = q0 + SB_BLOCK
        z = jnp.einsum('bhqd,bhkd->bhqk', q[:, :, q0:end], k[:, :, :end]).astype(jnp.float32) * scale
        q_pos = q0 + jnp.arange(SB_BLOCK)
        k_pos = jnp.arange(end)
        valid = k_pos[None, :] < q_pos[:, None]
        log_1m = jnp.where(valid, jax.nn.log_sigmoid(-z), 0.0)
        log_keep = lax.cumsum(log_1m, axis=3, reverse=True) - log_1m
        a = jnp.where(valid, jnp.exp(jax.nn.log_sigmoid(z) + log_keep), 0.0)
        outs.append(jnp.einsum('bhqk,bhkd->bhqd', a, vf[:, :, :end]))
    return jnp.concatenate(outs, axis=2).astype(v.dtype)


def gated_delta_rule_chunked(q, k, v, g, beta):
    B, H, T, dk = q.shape
    dv = v.shape[-1]
    n = T // GDN_CHUNK
    q = q * dk ** -0.5

    def chunks(a):
        return a.reshape(B, H, n, GDN_CHUNK, *a.shape[3:])

    q, k, v, g, beta = chunks(q), chunks(k), chunks(v), chunks(g), chunks(beta)
    gc = jnp.cumsum(g, axis=-1)
    idx = jnp.arange(GDN_CHUNK)
    incl = idx[:, None] >= idx[None, :]
    strict = idx[:, None] > idx[None, :]
    decay = jnp.where(incl, jnp.exp(jnp.where(incl, gc[..., :, None] - gc[..., None, :], 0.0)), 0.0)
    k_beta = k * beta[..., None]
    a_mat = jnp.where(strict, jnp.einsum('bhncd,bhnsd->bhncs', k_beta, k) * decay, 0.0)
    rhs = jnp.concatenate([v * beta[..., None], k_beta * jnp.exp(gc)[..., None]], axis=-1)
    sol = lax.linalg.triangular_solve(jnp.eye(GDN_CHUNK, dtype=a_mat.dtype) + a_mat, rhs,
                                      left_side=True, lower=True, unit_diagonal=True)
    u, w = sol[..., :dv], sol[..., dv:]
    qk = jnp.einsum('bhncd,bhnsd->bhncs', q, k) * decay
    q_dec = q * jnp.exp(gc)[..., None]
    k_dec = k * jnp.exp(gc[..., -1:] - gc)[..., None]
    g_last = jnp.exp(gc[..., -1])
    xs = (jnp.moveaxis(qk, 2, 0), jnp.moveaxis(u, 2, 0), jnp.moveaxis(w, 2, 0),
          jnp.moveaxis(q_dec, 2, 0), jnp.moveaxis(k_dec, 2, 0), jnp.moveaxis(g_last, 2, 0))

    def step(state, inp):
        qk_c, u_c, w_c, q_c, k_c, gl = inp
        v_new = u_c - jnp.einsum('bhcd,bhde->bhce', w_c, state)
        o = jnp.einsum('bhcd,bhde->bhce', q_c, state) + jnp.einsum('bhcs,bhse->bhce', qk_c, v_new)
        state = state * gl[..., None, None] + jnp.einsum('bhcd,bhce->bhde', k_c, v_new)
        return state, o

    s0 = jnp.zeros((B, H, dk, dv), jnp.float32)
    _, o = lax.scan(step, s0, xs)
    return jnp.moveaxis(o, 0, 2).reshape(B, H, T, dv)


def even_mixer(h, w_in, pool_w, pool_scale, w_out):
    B, T, _ = h.shape
    proj = h @ w_in
    u, q, k, v = jnp.split(proj, [POOL_WIDTH, POOL_WIDTH + SB_WIDTH, POOL_WIDTH + 2 * SB_WIDTH], axis=-1)
    pool_out = pool_mixer(u, pool_w, pool_scale)

    def heads(a):
        return a.reshape(B, T, SB_HEADS, SB_HEAD_DIM).transpose(0, 2, 1, 3)

    attn = stick_breaking_attention(heads(q), heads(k), heads(v))
    attn = attn.transpose(0, 2, 1, 3).reshape(B, T, SB_WIDTH)
    return jnp.concatenate([pool_out, attn], axis=-1) @ w_out


def odd_mixer(h, w_in, conv_w, a_log, dt_bias, norm_w, w_out):
    B, T, _ = h.shape
    proj = h @ w_in
    qkv, z, b, a = jnp.split(proj, [3 * MIX_WIDTH, 4 * MIX_WIDTH, 4 * MIX_WIDTH + GDN_HEADS], axis=-1)
    qkv = jax.nn.silu(causal_dwconv(qkv, conv_w))
    q, k, v = jnp.split(qkv, 3, axis=-1)

    def heads(x_):
        return x_.reshape(B, T, GDN_HEADS, GDN_HEAD_DIM).transpose(0, 2, 1, 3).astype(jnp.float32)

    q, k, v = l2norm(heads(q)), l2norm(heads(k)), heads(v)
    beta = jax.nn.sigmoid(b.astype(jnp.float32)).transpose(0, 2, 1)
    g = (-jnp.exp(a_log.astype(jnp.float32))
         * jax.nn.softplus(a.astype(jnp.float32) + dt_bias.astype(jnp.float32))).transpose(0, 2, 1)
    o = gated_delta_rule_chunked(q, k, v, g, beta).transpose(0, 2, 1, 3)
    o = o * lax.rsqrt(jnp.mean(o * o, axis=-1, keepdims=True) + EPS) * norm_w.astype(jnp.float32)
    o = o * jax.nn.silu(z.reshape(B, T, GDN_HEADS, GDN_HEAD_DIM).astype(jnp.float32))
    return o.reshape(B, T, MIX_WIDTH).astype(h.dtype) @ w_out


def conv_ffn(h, w_up, conv_w, w_down):
    up = causal_dwconv(h @ w_up, conv_w)
    gate, val = jnp.split(up, 2, axis=-1)
    return (jax.nn.silu(gate) * val) @ w_down


def setup_inputs(seed: int = 0) -> dict:
    key = jax.random.key(seed)
    ks = jax.random.split(key, 24)
    f32 = jnp.float32

    def dense(k_, shape, fan_in):
        return jax.random.normal(k_, shape, f32) * fan_in ** -0.5

    def gain(k_, shape):
        return 1.0 + 0.02 * jax.random.normal(k_, shape, f32)

    dt = jnp.exp(jax.random.uniform(ks[11], (N_ODD, GDN_HEADS), f32, np.log(1e-3), np.log(1e-1)))
    return {
        'x': jax.random.normal(ks[0], (BATCH, SEQ, D_MODEL), f32),
        'p': jax.random.normal(ks[1], (DEPTH, BATCH, SEQ, PLE_DIM), f32),
        'mix_norm_e': gain(ks[2], (N_EVEN, D_MODEL)),
        'w_in_e': dense(ks[3], (N_EVEN, D_MODEL, EVEN_IN), D_MODEL),
        'pool_w': dense(ks[4], (N_EVEN, N_POOL_GROUPS, POOL_GROUP_DIM, POOL_GROUP_DIM), POOL_GROUP_DIM),
        'pool_scale': 1.0 + 0.1 * jax.random.normal(ks[5], (N_EVEN, POOL_WIDTH), f32),
        'w_out_e': dense(ks[6], (N_EVEN, MIX_WIDTH, D_MODEL), MIX_WIDTH),
        'mix_norm_o': gain(ks[7], (N_ODD, D_MODEL)),
        'w_in_o': dense(ks[8], (N_ODD, D_MODEL, ODD_IN), D_MODEL),
        'conv_qkv_o': dense(ks[9], (N_ODD, GDN_CONV, 3 * MIX_WIDTH), GDN_CONV),
        'a_log_o': jnp.log(jax.random.uniform(ks[10], (N_ODD, GDN_HEADS), f32, 1.0, 16.0)),
        'dt_bias_o': dt + jnp.log(-jnp.expm1(-dt)),
        'gdn_norm_o': gain(ks[12], (N_ODD, GDN_HEAD_DIM)),
        'w_out_o': dense(ks[13], (N_ODD, MIX_WIDTH, D_MODEL), MIX_WIDTH),
        'ffn_norm': gain(ks[14], (DEPTH, D_MODEL)),
        'w_up': dense(ks[15], (DEPTH, D_MODEL, 2 * FFN_DIM), D_MODEL),
        'ffn_conv': dense(ks[16], (DEPTH, FFN_CONV, 2 * FFN_DIM), FFN_CONV),
        'w_down': dense(ks[17], (DEPTH, FFN_DIM, D_MODEL), FFN_DIM),
        'ple_norm': gain(ks[18], (DEPTH, D_MODEL)),
        'w_ple_gate': dense(ks[19], (DEPTH, D_MODEL, D_MODEL), D_MODEL),
        'w_ple': dense(ks[20], (DEPTH, PLE_DIM, D_MODEL), PLE_DIM),
        'final_norm': gain(ks[21], (D_MODEL,)),
    }


def reference(x, p, mix_norm_e, w_in_e, pool_w, pool_scale, w_out_e,
              mix_norm_o, w_in_o, conv_qkv_o, a_log_o, dt_bias_o, gdn_norm_o, w_out_o,
              ffn_norm, w_up, ffn_conv, w_down, ple_norm, w_ple_gate, w_ple, final_norm):
    for i in range(DEPTH):
        j = i // 2
        if i % 2 == 0:
            x = x + even_mixer(rmsnorm(x, mix_norm_e[j]), w_in_e[j], pool_w[j], pool_scale[j], w_out_e[j])
        else:
            x = x + odd_mixer(rmsnorm(x, mix_norm_o[j]), w_in_o[j], conv_qkv_o[j], a_log_o[j],
                              dt_bias_o[j], gdn_norm_o[j], w_out_o[j])
        x = x + conv_ffn(rmsnorm(x, ffn_norm[i]), w_up[i], ffn_conv[i], w_down[i])
        gate = jax.nn.sigmoid(rmsnorm(x, ple_norm[i]) @ w_ple_gate[i])
        x = x + (p[i] @ w_ple[i]) * gate
    return rmsnorm(x, final_norm)
```

```python
import functools

import jax
import jax.numpy as jnp
from jax import lax
from jax.experimental import pallas as pl
from jax.experimental.pallas import tpu as pltpu

F32 = jnp.float32
BF16 = jnp.bfloat16

EPS = 1e-6
LANES = 128
POOL_WINDOWS = (2, 4, 8, 16)
POOL_HALO = 16
SB_HEAD_DIM = 64
SB_TILE = 128
SB_LOG_UNDERFLOW = 88.0
GDN_HEAD_DIM = 128
GDN_CHUNK = 64
GDN_TILE = 256
GDN_CONV = 4
GDN_HALO = 16
FFN_CONV = 3
CONV_HALO = 8
ROW_TILE = 512
FFN_CHUNK = 256
VMEM_LIMIT_BYTES = 56 * 1024 * 1024


def _params(*sem):
    return pltpu.CompilerParams(dimension_semantics=sem, vmem_limit_bytes=VMEM_LIMIT_BYTES)


def _const_spec(shape):
    nd = len(shape)
    return pl.BlockSpec(shape, lambda *_: (0,) * nd, pipeline_mode=pl.Buffered(1))


def _rms(x, gain):
    return x * lax.rsqrt(jnp.mean(x * x, axis=-1, keepdims=True) + EPS) * gain


def _dot(a, b):
    return jnp.dot(a, b, preferred_element_type=F32)


def _dot_nt(a, b):
    return lax.dot_general(a, b, (((1,), (1,)), ((), ())), preferred_element_type=F32)


def _dot_tn(a, b):
    return lax.dot_general(a, b, (((0,), (0,)), ((), ())), preferred_element_type=F32)


def _split_bf16(x):
    hi = x.astype(BF16)
    lo = (x - hi.astype(F32)).astype(BF16)
    return hi, lo


def _norm_matmul_kernel(x_ref, g_ref, w_ref, o_ref, *, n_chunk):
    xn = _rms(x_ref[...], g_ref[...]).astype(BF16)
    for c in range(0, w_ref.shape[1], n_chunk):
        o_ref[:, c:c + n_chunk] = _dot(xn, w_ref[:, c:c + n_chunk]).astype(o_ref.dtype)


def _norm_matmul(x, gain, w, *, tm=ROW_TILE, n_chunk=512):
    m, d = x.shape
    n = w.shape[1]
    return pl.pallas_call(
        functools.partial(_norm_matmul_kernel, n_chunk=n_chunk),
        grid=(m // tm,),
        in_specs=[pl.BlockSpec((tm, d), lambda r: (r, 0)), _const_spec((1, d)), _const_spec((d, n))],
        out_specs=pl.BlockSpec((tm, n), lambda r: (r, 0)),
        out_shape=jax.ShapeDtypeStruct((m, n), BF16),
        compiler_params=_params("parallel"),
        name="norm_matmul",
    )(x, gain, w)


def _odd_in_kernel(x_ref, g_ref, w_ref, wba_ref, alog_ref, dt_ref, o_ref, gb_ref, *, n_chunk, n_heads):
    xn = _rms(x_ref[...], g_ref[...]).astype(BF16)
    for c in range(0, w_ref.shape[1], n_chunk):
        o_ref[:, c:c + n_chunk] = _dot(xn, w_ref[:, c:c + n_chunk]).astype(o_ref.dtype)
    ba = _dot(xn, wba_ref[...])
    lane = lax.broadcasted_iota(jnp.int32, ba.shape, 1)
    g = -jnp.exp(alog_ref[...]) * jax.nn.softplus(ba + dt_ref[...])
    gb_ref[...] = jnp.where(lane < n_heads, jax.nn.sigmoid(ba), g)


def _odd_in_proj(x, gain, w, wba, alog_lane, dt_lane, n_heads, *, tm=ROW_TILE, n_chunk=512):
    m, d = x.shape
    n = w.shape[1]
    return pl.pallas_call(
        functools.partial(_odd_in_kernel, n_chunk=n_chunk, n_heads=n_heads),
        grid=(m // tm,),
        in_specs=[pl.BlockSpec((tm, d), lambda r: (r, 0)), _const_spec((1, d)), _const_spec((d, n)),
                  _const_spec((d, LANES)), _const_spec((1, LANES)), _const_spec((1, LANES))],
        out_specs=[pl.BlockSpec((tm, n), lambda r: (r, 0)), pl.BlockSpec((tm, LANES), lambda r: (r, 0))],
        out_shape=[jax.ShapeDtypeStruct((m, n), BF16), jax.ShapeDtypeStruct((m, LANES), F32)],
        compiler_params=_params("parallel"),
        name="odd_in_proj",
    )(x, gain, w, wba, alog_lane, dt_lane)


def _sb_attn_kernel(q_ref, k_ref, v_ref, o_ref, acc_ref, c_ref):
    t = SB_TILE
    seq = q_ref.shape[1]
    lane = lax.broadcasted_iota(jnp.int32, (t, LANES), 1)
    head0 = lane < SB_HEAD_DIM
    row = lax.broadcasted_iota(jnp.int32, (t, t), 0)
    col = lax.broadcasted_iota(jnp.int32, (t, t), 1)
    r2 = lax.broadcasted_iota(jnp.int32, (2 * t, 2 * t), 0) % t
    c2 = lax.broadcasted_iota(jnp.int32, (2 * t, 2 * t), 1)
    cum = jnp.where((c2 >= t) | (r2 > c2), 1.0, 0.0).astype(BF16)

    def q_tile(i, carry):
        q = q_ref[0, pl.ds(pl.multiple_of(i * t, t), t), :]
        qh = (jnp.where(head0, q, jnp.zeros_like(q)), jnp.where(head0, jnp.zeros_like(q), q))
        acc_ref[...] = jnp.zeros_like(acc_ref)
        c_ref[...] = jnp.zeros_like(c_ref)

        def cond(st):
            j, top = st
            return jnp.logical_and(j >= 0, top > -SB_LOG_UNDERFLOW)

        def body(st):
            j, _ = st
            ks = pl.ds(pl.multiple_of(j * t, t), t)
            kb = k_ref[0, ks, :]
            vb = v_ref[0, ks, :]
            vh = (jnp.where(head0, vb, jnp.zeros_like(vb)), jnp.where(head0, jnp.zeros_like(vb), vb))
            valid = (col - row) < (i - j) * t
            acc = acc_ref[...]
            top = jnp.float32(-jnp.inf)
            for h in range(2):
                z = _dot_nt(qh[h], kb)
                soft = jnp.log1p(jnp.exp(-jnp.abs(z)))
                ls_pos = jnp.minimum(z, 0.0) - soft
                ls_neg = jnp.where(valid, jnp.minimum(-z, 0.0) - soft, 0.0)
                hi, lo = _split_bf16(ls_neg)
                run = _dot(jnp.concatenate([hi, lo], axis=1), cum)
                c_old = c_ref[h]
                a = jnp.where(valid, jnp.exp(ls_pos + run[:, :t] + c_old), 0.0)
                acc = acc + _dot(a.astype(BF16), vh[h])
                c_new = c_old + run[:, t:]
                c_ref[h] = c_new
                top = jnp.maximum(top, jnp.max(c_new))
            acc_ref[...] = acc
            return j - 1, top

        lax.while_loop(cond, body, (i, jnp.float32(0.0)))
        o_ref[0, pl.ds(pl.multiple_of(i * t, t), t), :] = acc_ref[...].astype(o_ref.dtype)
        return carry

    lax.fori_loop(0, seq // t, q_tile, 0)


def _sb_attention(proj, n_batch, seq, q_col, k_col, v_col, n_pairs):
    blk = (1, seq, LANES)
    return pl.pallas_call(
        _sb_attn_kernel,
        grid=(n_batch, n_pairs),
        in_specs=[pl.BlockSpec(blk, lambda b, p: (b, 0, q_col + p)),
                  pl.BlockSpec(blk, lambda b, p: (b, 0, k_col + p)),
                  pl.BlockSpec(blk, lambda b, p: (b, 0, v_col + p))],
        out_specs=pl.BlockSpec(blk, lambda b, p: (b, 0, p)),
        out_shape=jax.ShapeDtypeStruct((n_batch, seq, n_pairs * LANES), BF16),
        scratch_shapes=[pltpu.VMEM((SB_TILE, LANES), F32), pltpu.VMEM((2, SB_TILE, LANES), F32)],
        compiler_params=_params("parallel", "parallel"),
        name="sb_attention",
    )(proj, proj, proj)


def _even_out_kernel(x_ref, u_ref, halo_ref, attn_ref, pw_ref, ps_ref, wo_ref, o_ref, s_ref, *, seq):
    tm = x_ref.shape[0]
    t0 = (pl.program_id(0) * tm) % seq
    u = u_ref[...].astype(F32)
    halo = halo_ref[...].astype(F32)
    s_ref[0:POOL_HALO, :] = jnp.where(t0 == 0, jnp.zeros_like(halo), halo)
    s_ref[POOL_HALO:, :] = u
    tpos = t0 + lax.broadcasted_iota(jnp.int32, (tm, 1), 0)
    parts = []
    for g, win in enumerate(POOL_WINDOWS):
        cs = slice(g * LANES, (g + 1) * LANES)
        ug = u[:, cs]
        ws = ug
        for k in range(1, win):
            ws = ws + s_ref[POOL_HALO - k:POOL_HALO - k + tm, cs]
        count = jnp.minimum(tpos + 1, win).astype(F32)
        y = ws / count - ug
        parts.append((_dot(y.astype(BF16), pw_ref[g]) * ps_ref[:, cs]).astype(BF16))
    cat = jnp.concatenate(parts + [attn_ref[...]], axis=1)
    o_ref[...] = x_ref[...] + _dot(cat, wo_ref[...])


def _even_out(x, proj, attn, pool_w, pool_scale, w_out, seq, *, tm=ROW_TILE):
    m, d = x.shape
    pw = len(POOL_WINDOWS) * LANES
    hb = tm // POOL_HALO
    return pl.pallas_call(
        functools.partial(_even_out_kernel, seq=seq),
        grid=(m // tm,),
        in_specs=[pl.BlockSpec((tm, d), lambda r: (r, 0)),
                  pl.BlockSpec((tm, pw), lambda r: (r, 0)),
                  pl.BlockSpec((POOL_HALO, pw), lambda r: (jnp.maximum(r * hb - 1, 0), 0)),
                  pl.BlockSpec((tm, attn.shape[1]), lambda r: (r, 0)),
                  _const_spec(pool_w.shape), _const_spec(pool_scale.shape), _const_spec(w_out.shape)],
        out_specs=pl.BlockSpec((tm, d), lambda r: (r, 0)),
        out_shape=jax.ShapeDtypeStruct((m, d), F32),
        scratch_shapes=[pltpu.VMEM((tm + POOL_HALO, pw), F32)],
        compiler_params=_params("parallel"),
        name="even_out",
    )(x, proj, proj, attn, pool_w, pool_scale, w_out)


def _ffn_ple_kernel(x_ref, p_ref, fg_ref, wup_ref, cw_ref, wdn_ref, pg_ref, wg_ref, wp_ref, fn_ref, o_ref,
                    acc_ref, ybuf_ref, carry_ref, *, seq, tf, final_norm):
    tm = x_ref.shape[0]
    ffn = wdn_ref.shape[0]
    first = (pl.program_id(0) * tm) % seq == 0
    x = x_ref[...]
    xn = _rms(x, fg_ref[...]).astype(BF16)
    for c in range(ffn // tf):
        conv = []
        for part in range(2):
            cs = slice(part * ffn + c * tf, part * ffn + (c + 1) * tf)
            y = _dot(xn, wup_ref[:, cs])
            prev = carry_ref[:, cs]
            ybuf_ref[part, 0:CONV_HALO, :] = jnp.where(first, jnp.zeros_like(prev), prev)
            ybuf_ref[part, CONV_HALO:, :] = y
            carry_ref[:, cs] = y[tm - CONV_HALO:, :]
            w = cw_ref[:, cs]
            conv.append(w[2:3] * y
                        + w[1:2] * ybuf_ref[part, CONV_HALO - 1:CONV_HALO - 1 + tm, :]
                        + w[0:1] * ybuf_ref[part, CONV_HALO - 2:CONV_HALO - 2 + tm, :])
        h = (conv[0] * jax.nn.sigmoid(conv[0]) * conv[1]).astype(BF16)
        down = _dot(h, wdn_ref[c * tf:(c + 1) * tf, :])
        if c == 0:
            acc_ref[...] = down
        else:
            acc_ref[...] += down
    x2 = x + acc_ref[...]
    gate = jax.nn.sigmoid(_dot(_rms(x2, pg_ref[...]).astype(BF16), wg_ref[...]))
    x3 = x2 + _dot(p_ref[...].astype(BF16), wp_ref[...]) * gate
    if final_norm:
        x3 = _rms(x3, fn_ref[...])
    o_ref[...] = x3


def _ffn_ple(x, p, ffn_gain, w_up, conv_w, w_down, ple_gain, w_gate, w_ple, final_gain, seq, final_norm,
             *, tm=ROW_TILE, tf=FFN_CHUNK):
    m, d = x.shape
    row = lambda r: (r, 0)
    return pl.pallas_call(
        functools.partial(_ffn_ple_kernel, seq=seq, tf=tf, final_norm=final_norm),
        grid=(m // tm,),
        in_specs=[pl.BlockSpec((tm, d), row), pl.BlockSpec((tm, p.shape[1]), row),
                  _const_spec((1, d)), _const_spec(w_up.shape), _const_spec(conv_w.shape),
                  _const_spec(w_down.shape), _const_spec((1, d)), _const_spec(w_gate.shape),
                  _const_spec(w_ple.shape), _const_spec((1, d))],
        out_specs=pl.BlockSpec((tm, d), row),
        out_shape=jax.ShapeDtypeStruct((m, d), F32),
        scratch_shapes=[pltpu.VMEM((tm, d), F32), pltpu.VMEM((2, tm + CONV_HALO, tf), F32),
                        pltpu.VMEM((CONV_HALO, w_up.shape[1]), F32)],
        compiler_params=_params("arbitrary"),
        name="ffn_ple",
    )(x, p, ffn_gain, w_up, conv_w, w_down, ple_gain, w_gate, w_ple, final_gain)


def _gdn_kernel(q_ref, k_ref, v_ref, z_ref, gb_ref, cq_ref, ck_ref, cv_ref, nw_ref, o_ref,
                cbuf_ref, u_ref, w_ref, qd_ref, kd_ref, qk_ref, gl_ref, vn_ref, s_ref, *, n_heads, hg):
    seq = q_ref.shape[1]
    tile, chunk, dh = GDN_TILE, GDN_CHUNK, GDN_HEAD_DIM
    cpt = tile // chunk
    row = lax.broadcasted_iota(jnp.int32, (tile, tile), 0)
    col = lax.broadcasted_iota(jnp.int32, (tile, tile), 1)
    same = (row // chunk) == (col // chunk)
    incl = jnp.logical_and(same, row >= col)
    strict = jnp.logical_and(same, row > col)
    tri = jnp.where(incl, 1.0, 0.0).astype(BF16)
    tri2 = jnp.concatenate([tri, tri], axis=1)
    lane = lax.broadcasted_iota(jnp.int32, (tile, LANES), 1)

    def conv_silu(src_ref, cw_ref, r0, has_prev, hl):
        raw = src_ref[0, pl.ds(r0, tile), hl].astype(F32)
        h0 = pl.multiple_of(jnp.maximum(r0 - GDN_HALO, 0), GDN_HALO)
        halo = src_ref[0, pl.ds(h0, GDN_HALO), hl].astype(F32)
        cbuf_ref[0:GDN_HALO, :] = jnp.where(has_prev, halo, jnp.zeros_like(halo))
        cbuf_ref[GDN_HALO:, :] = raw
        w = cw_ref[:, hl]
        y = w[GDN_CONV - 1:GDN_CONV] * raw
        for i in range(1, GDN_CONV):
            y = y + w[GDN_CONV - 1 - i:GDN_CONV - i] * cbuf_ref[GDN_HALO - i:GDN_HALO - i + tile, :]
        return y * jax.nn.sigmoid(y)

    def l2n(a):
        return a * lax.rsqrt(jnp.sum(a * a, axis=-1, keepdims=True) + EPS)

    for h in range(hg):
        hl = slice(h * dh, (h + 1) * dh)
        head = pl.program_id(1) * hg + h

        def pre(ti, carry, h=h, hl=hl, head=head):
            r0 = pl.multiple_of(ti * tile, tile)
            rows = pl.ds(r0, tile)
            has_prev = ti > 0
            q = l2n(conv_silu(q_ref, cq_ref, r0, has_prev, hl)) * (dh ** -0.5)
            k = l2n(conv_silu(k_ref, ck_ref, r0, has_prev, hl))
            v = conv_silu(v_ref, cv_ref, r0, has_prev, hl)
            gbt = gb_ref[0, rows, :]
            beta = jnp.sum(jnp.where(lane == head, gbt, 0.0), axis=-1, keepdims=True)
            g = jnp.sum(jnp.where(lane == n_heads + head, gbt, 0.0), axis=-1, keepdims=True)
            g_hi, g_lo = _split_bf16(jnp.broadcast_to(g, (tile, dh)))
            gc = _dot(tri2, jnp.concatenate([g_hi, g_lo], axis=0))
            gc_last = jnp.broadcast_to(gc.reshape(cpt, chunk, dh)[:, chunk - 1:chunk, :],
                                       (cpt, chunk, dh)).reshape(tile, dh)
            gc_row = gc.T[0:1, :]
            diff = jnp.concatenate([gc] * (tile // dh), axis=1) - gc_row
            decay = jnp.where(incl, jnp.exp(jnp.where(incl, diff, 0.0)), 0.0)
            kb16 = k.astype(BF16)
            k_beta = k * beta
            a_mat = jnp.where(strict, _dot_nt(k_beta.astype(BF16), kb16) * decay, 0.0)
            qk = _dot_nt(q.astype(BF16), kb16) * decay
            pw = -a_mat
            res = pw
            step = 1
            while step < chunk // 2:
                pw16 = pw.astype(BF16)
                pw = _dot(pw16, pw16)
                res = res + pw + _dot(res.astype(BF16), pw.astype(BF16))
                step *= 2
            egc = jnp.exp(gc)
            rhs = jnp.concatenate([v * beta, k_beta * egc], axis=1)
            sol = rhs + _dot(res.astype(BF16), rhs.astype(BF16))
            u_ref[h, rows, :] = sol[:, :dh]
            w_ref[h, rows, :] = sol[:, dh:].astype(BF16)
            qd_ref[h, rows, :] = (q * egc).astype(BF16)
            kd_ref[h, rows, :] = (k * jnp.exp(gc_last - gc)).astype(BF16)
            qk_ref[h, rows, :] = qk.astype(BF16)
            gl_ref[h, rows, :] = jnp.exp(gc_last)
            return carry

        lax.fori_loop(0, seq // tile, pre, 0)

    vn_ref[...] = jnp.zeros_like(vn_ref)
    s_ref[...] = jnp.zeros_like(s_ref)

    def scan(n, carry):
        r0 = pl.multiple_of(n * chunk, chunk)
        rows = pl.ds(r0, chunk)
        slot = pl.ds(pl.multiple_of((n % cpt) * chunk, chunk), chunk)
        for h in range(hg):
            hl = slice(h * dh, (h + 1) * dh)
            state = s_ref[h]
            wq = jnp.concatenate([w_ref[h, rows, :], qd_ref[h, rows, :]], axis=0)
            ws = _dot(wq, state.astype(BF16))
            v_new = u_ref[h, rows, :] - ws[:chunk]
            v16 = v_new.astype(BF16)
            vn_ref[h, slot, :] = v16
            o = ws[chunk:] + _dot(qk_ref[h, rows, :], vn_ref[h])
            gl = gl_ref[h, rows, :]
            s_ref[h] = state * jnp.concatenate([gl, gl], axis=0) + _dot_tn(kd_ref[h, rows, :], v16)
            o = o * lax.rsqrt(jnp.mean(o * o, axis=-1, keepdims=True) + EPS) * nw_ref[...]
            zg = z_ref[0, rows, hl].astype(F32)
            o_ref[0, rows, hl] = (o * (zg * jax.nn.sigmoid(zg))).astype(o_ref.dtype)
        return carry

    lax.fori_loop(0, seq // chunk, scan, 0)


def _gdn(proj, gb, conv_w, norm_w, n_batch, seq, n_heads, *, hg=2):
    width = hg * GDN_HEAD_DIM
    groups = n_heads // hg
    blk = (1, seq, width)
    col = lambda base: (lambda b, p: (b, 0, base * groups + p))
    ccol = lambda base: (lambda b, p: (0, base * groups + p))
    scr = lambda cols, dt: pltpu.VMEM((hg, seq, cols), dt)
    return pl.pallas_call(
        functools.partial(_gdn_kernel, n_heads=n_heads, hg=hg),
        grid=(n_batch, groups),
        in_specs=[pl.BlockSpec(blk, col(0)), pl.BlockSpec(blk, col(1)), pl.BlockSpec(blk, col(2)),
                  pl.BlockSpec(blk, col(3)),
                  pl.BlockSpec((1, seq, LANES), lambda b, p: (b, 0, 0)),
                  pl.BlockSpec((GDN_CONV, width), ccol(0)), pl.BlockSpec((GDN_CONV, width), ccol(1)),
                  pl.BlockSpec((GDN_CONV, width), ccol(2)),
                  _const_spec((1, GDN_HEAD_DIM))],
        out_specs=pl.BlockSpec(blk, lambda b, p: (b, 0, p)),
        out_shape=jax.ShapeDtypeStruct((n_batch, seq, n_heads * GDN_HEAD_DIM), BF16),
        scratch_shapes=[pltpu.VMEM((GDN_TILE + GDN_HALO, GDN_HEAD_DIM), F32),
                        scr(GDN_HEAD_DIM, F32), scr(GDN_HEAD_DIM, BF16), scr(GDN_HEAD_DIM, BF16),
                        scr(GDN_HEAD_DIM, BF16), scr(GDN_TILE, BF16), scr(GDN_HEAD_DIM, F32),
                        pltpu.VMEM((hg, GDN_TILE, GDN_HEAD_DIM), BF16),
                        pltpu.VMEM((hg, GDN_HEAD_DIM, GDN_HEAD_DIM), F32)],
        compiler_params=_params("parallel", "parallel"),
        name="gdn",
    )(proj, proj, proj, proj, gb, conv_w, conv_w, conv_w, norm_w)


def _matmul_residual_kernel(x_ref, a_ref, w_ref, o_ref):
    o_ref[...] = x_ref[...] + _dot(a_ref[...], w_ref[...])


def _matmul_residual(x, a, w, *, tm=ROW_TILE):
    m, d = x.shape
    return pl.pallas_call(
        _matmul_residual_kernel,
        grid=(m // tm,),
        in_specs=[pl.BlockSpec((tm, d), lambda r: (r, 0)), pl.BlockSpec((tm, a.shape[1]), lambda r: (r, 0)),
                  _const_spec(w.shape)],
        out_specs=pl.BlockSpec((tm, d), lambda r: (r, 0)),
        out_shape=jax.ShapeDtypeStruct((m, d), F32),
        compiler_params=_params("parallel"),
        name="matmul_residual",
    )(x, a, w)


def kernel(x, p, mix_norm_e, w_in_e, pool_w, pool_scale, w_out_e, mix_norm_o, w_in_o, conv_qkv_o, a_log_o,
           dt_bias_o, gdn_norm_o, w_out_o, ffn_norm, w_up, ffn_conv, w_down, ple_norm, w_ple_gate, w_ple,
           final_norm):
    n_batch, seq, d = x.shape
    depth = p.shape[0]
    m = n_batch * seq
    pool_width = len(POOL_WINDOWS) * LANES
    row = lambda a: a.reshape(1, -1)
    xf = x.reshape(m, d)
    for i in range(depth):
        j = i // 2
        if i % 2 == 0:
            sb_width = (w_in_e.shape[2] - pool_width) // 3
            scale = jnp.concatenate([jnp.ones((pool_width,), F32),
                                     jnp.full((sb_width,), SB_HEAD_DIM ** -0.5, F32),
                                     jnp.ones((2 * sb_width,), F32)])
            proj = _norm_matmul(xf, row(mix_norm_e[j]), (w_in_e[j] * scale).astype(BF16))
            pc, sc = pool_width // LANES, sb_width // LANES
            attn = _sb_attention(proj.reshape(n_batch, seq, -1), n_batch, seq, pc, pc + sc, pc + 2 * sc, sc)
            xf = _even_out(xf, proj, attn.reshape(m, sb_width), pool_w[j].astype(BF16), row(pool_scale[j]),
                           w_out_e[j].astype(BF16), seq)
        else:
            n_heads = a_log_o.shape[1]
            mix = n_heads * GDN_HEAD_DIM
            wba = jnp.pad(w_in_o[j][:, 4 * mix:], ((0, 0), (0, LANES - 2 * n_heads))).astype(BF16)
            lane_pad = lambda a: jnp.pad(a, (n_heads, LANES - 2 * n_heads)).reshape(1, LANES)
            proj, gb = _odd_in_proj(xf, row(mix_norm_o[j]), w_in_o[j][:, :4 * mix].astype(BF16), wba,
                                    lane_pad(a_log_o[j]), lane_pad(dt_bias_o[j]), n_heads)
            o = _gdn(proj.reshape(n_batch, seq, -1), gb.reshape(n_batch, seq, LANES), conv_qkv_o[j],
                     row(gdn_norm_o[j]), n_batch, seq, n_heads)
            xf = _matmul_residual(xf, o.reshape(m, mix), w_out_o[j].astype(BF16))
        xf = _ffn_ple(xf, p[i].reshape(m, -1), row(ffn_norm[i]), w_up[i].astype(BF16), ffn_conv[i],
                      w_down[i].astype(BF16), row(ple_norm[i]), w_ple_gate[i].astype(BF16),
                      w_ple[i].astype(BF16), row(final_norm), seq, i == depth - 1)
    return xf.reshape(n_batch, seq, d)
```

```python
import functools

import jax
import jax.numpy as jnp
from jax import lax
from jax.experimental import pallas as pl
from jax.experimental.pallas import tpu as pltpu

F32 = jnp.float32
BF16 = jnp.bfloat16

EPS = 1e-6
LANES = 128
POOL_WINDOWS = (2, 4, 8, 16)
POOL_HALO = 16
SB_HEAD_DIM = 64
SB_TILE = 128
SB_LOG_UNDERFLOW = 88.0
GDN_HEAD_DIM = 128
GDN_CHUNK = 64
GDN_TILE = 256
GDN_TILES_PER_STEP = 2
GDN_CONV = 4
GDN_HALO = 16
FFN_CONV = 3
CONV_HALO = 8
ROW_TILE = 512
FFN_CHUNK = 256
VMEM_LIMIT_BYTES = 56 * 1024 * 1024


def _params(*sem):
    return pltpu.CompilerParams(dimension_semantics=sem, vmem_limit_bytes=VMEM_LIMIT_BYTES)


def _const_spec(shape):
    nd = len(shape)
    return pl.BlockSpec(shape, lambda *_: (0,) * nd, pipeline_mode=pl.Buffered(1))


def _rms(x, gain):
    return x * lax.rsqrt(jnp.mean(x * x, axis=-1, keepdims=True) + EPS) * gain


def _dot(a, b):
    return jnp.dot(a, b, preferred_element_type=F32)


def _dot_nt(a, b):
    return lax.dot_general(a, b, (((1,), (1,)), ((), ())), preferred_element_type=F32)


def _dot_tn(a, b):
    return lax.dot_general(a, b, (((0,), (0,)), ((), ())), preferred_element_type=F32)


def _split_bf16(x):
    hi = x.astype(BF16)
    lo = (x - hi.astype(F32)).astype(BF16)
    return hi, lo


def _norm_matmul_kernel(x_ref, g_ref, w_ref, o_ref, *, n_chunk):
    xn = _rms(x_ref[...], g_ref[...]).astype(BF16)
    for c in range(0, w_ref.shape[1], n_chunk):
        o_ref[:, c:c + n_chunk] = _dot(xn, w_ref[:, c:c + n_chunk]).astype(o_ref.dtype)


def _norm_matmul(x, gain, w, *, tm=ROW_TILE, n_chunk=512):
    m, d = x.shape
    n = w.shape[1]
    return pl.pallas_call(
        functools.partial(_norm_matmul_kernel, n_chunk=n_chunk),
        grid=(m // tm,),
        in_specs=[pl.BlockSpec((tm, d), lambda r: (r, 0)), _const_spec((1, d)), _const_spec((d, n))],
        out_specs=pl.BlockSpec((tm, n), lambda r: (r, 0)),
        out_shape=jax.ShapeDtypeStruct((m, n), BF16),
        compiler_params=_params("parallel"),
        name="norm_matmul",
    )(x, gain, w)


def _odd_in_kernel(x_ref, g_ref, w_ref, wba_ref, alog_ref, dt_ref, o_ref, gb_ref, *, n_chunk, n_heads):
    xn = _rms(x_ref[...], g_ref[...]).astype(BF16)
    for c in range(0, w_ref.shape[1], n_chunk):
        o_ref[:, c:c + n_chunk] = _dot(xn, w_ref[:, c:c + n_chunk]).astype(o_ref.dtype)
    ba = _dot(xn, wba_ref[...])
    lane = lax.broadcasted_iota(jnp.int32, ba.shape, 1)
    is_g = jnp.logical_and(lane >= n_heads, lane < 2 * n_heads)
    g = jnp.where(is_g, -jnp.exp(alog_ref[...]) * jax.nn.softplus(ba + dt_ref[...]), 0.0)
    beta = jax.nn.sigmoid(ba)
    sub = GDN_TILE
    row = lax.broadcasted_iota(jnp.int32, (sub, sub), 0)
    col = lax.broadcasted_iota(jnp.int32, (sub, sub), 1)
    tri = jnp.where(jnp.logical_and(row // GDN_CHUNK == col // GDN_CHUNK, row >= col), 1.0, 0.0).astype(BF16)
    tri2 = jnp.concatenate([tri, tri], axis=1)
    is_beta = lax.broadcasted_iota(jnp.int32, (sub, LANES), 1) < n_heads
    for r in range(0, ba.shape[0], sub):
        hi, lo = _split_bf16(g[r:r + sub])
        gc = _dot(tri2, jnp.concatenate([hi, lo], axis=0))
        gb_ref[r:r + sub, :] = jnp.where(is_beta, beta[r:r + sub], gc)


def _odd_in_proj(x, gain, w, wba, alog_lane, dt_lane, n_heads, *, tm=ROW_TILE, n_chunk=512):
    m, d = x.shape
    n = w.shape[1]
    return pl.pallas_call(
        functools.partial(_odd_in_kernel, n_chunk=n_chunk, n_heads=n_heads),
        grid=(m // tm,),
        in_specs=[pl.BlockSpec((tm, d), lambda r: (r, 0)), _const_spec((1, d)), _const_spec((d, n)),
                  _const_spec((d, LANES)), _const_spec((1, LANES)), _const_spec((1, LANES))],
        out_specs=[pl.BlockSpec((tm, n), lambda r: (r, 0)), pl.BlockSpec((tm, LANES), lambda r: (r, 0))],
        out_shape=[jax.ShapeDtypeStruct((m, n), BF16), jax.ShapeDtypeStruct((m, LANES), F32)],
        compiler_params=_params("parallel"),
        name="odd_in_proj",
    )(x, gain, w, wba, alog_lane, dt_lane)


def _sb_attn_kernel(q_ref, k_ref, v_ref, o_ref, acc_ref, c_ref):
    t = SB_TILE
    seq = q_ref.shape[1]
    lane = lax.broadcasted_iota(jnp.int32, (t, LANES), 1)
    head0 = lane < SB_HEAD_DIM
    row = lax.broadcasted_iota(jnp.int32, (t, t), 0)
    col = lax.broadcasted_iota(jnp.int32, (t, t), 1)
    r2 = lax.broadcasted_iota(jnp.int32, (2 * t, 2 * t), 0) % t
    c2 = lax.broadcasted_iota(jnp.int32, (2 * t, 2 * t), 1)
    cum = jnp.where((c2 >= t) | (r2 > c2), 1.0, 0.0).astype(BF16)

    def q_tile(i, carry):
        q = q_ref[0, pl.ds(pl.multiple_of(i * t, t), t), :]
        qh = (jnp.where(head0, q, jnp.zeros_like(q)), jnp.where(head0, jnp.zeros_like(q), q))
        acc_ref[...] = jnp.zeros_like(acc_ref)
        c_ref[...] = jnp.zeros_like(c_ref)

        def cond(st):
            j, top = st
            return jnp.logical_and(j >= 0, top > -SB_LOG_UNDERFLOW)

        def body(st):
            j, _ = st
            ks = pl.ds(pl.multiple_of(j * t, t), t)
            kb = k_ref[0, ks, :]
            vb = v_ref[0, ks, :]
            vh = (jnp.where(head0, vb, jnp.zeros_like(vb)), jnp.where(head0, jnp.zeros_like(vb), vb))
            valid = (col - row) < (i - j) * t
            acc = acc_ref[...]
            top = jnp.float32(-jnp.inf)
            for h in range(2):
                z = _dot_nt(qh[h], kb)
                soft = jnp.log1p(jnp.exp(-jnp.abs(z)))
                ls_pos = jnp.minimum(z, 0.0) - soft
                ls_neg = jnp.where(valid, jnp.minimum(-z, 0.0) - soft, 0.0)
                hi, lo = _split_bf16(ls_neg)
                run = _dot(jnp.concatenate([hi, lo], axis=1), cum)
                c_old = c_ref[h]
                a = jnp.where(valid, jnp.exp(ls_pos + run[:, :t] + c_old), 0.0)
                acc = acc + _dot(a.astype(BF16), vh[h])
                c_new = c_old + run[:, t:]
                c_ref[h] = c_new
                top = jnp.maximum(top, jnp.max(c_new))
            acc_ref[...] = acc
            return j - 1, top

        lax.while_loop(cond, body, (i, jnp.float32(0.0)))
        o_ref[0, pl.ds(pl.multiple_of(i * t, t), t), :] = acc_ref[...].astype(o_ref.dtype)
        return carry

    lax.fori_loop(0, seq // t, q_tile, 0)


def _sb_attention(proj, n_batch, seq, q_col, k_col, v_col, n_pairs):
    blk = (1, seq, LANES)
    return pl.pallas_call(
        _sb_attn_kernel,
        grid=(n_batch, n_pairs),
        in_specs=[pl.BlockSpec(blk, lambda b, p: (b, 0, q_col + p)),
                  pl.BlockSpec(blk, lambda b, p: (b, 0, k_col + p)),
                  pl.BlockSpec(blk, lambda b, p: (b, 0, v_col + p))],
        out_specs=pl.BlockSpec(blk, lambda b, p: (b, 0, p)),
        out_shape=jax.ShapeDtypeStruct((n_batch, seq, n_pairs * LANES), BF16),
        scratch_shapes=[pltpu.VMEM((SB_TILE, LANES), F32), pltpu.VMEM((2, SB_TILE, LANES), F32)],
        compiler_params=_params("parallel", "parallel"),
        name="sb_attention",
    )(proj, proj, proj)


def _even_out_kernel(x_ref, u_ref, halo_ref, attn_ref, pw_ref, ps_ref, wo_ref, o_ref, s_ref, *, seq):
    tm = x_ref.shape[0]
    t0 = (pl.program_id(0) * tm) % seq
    u = u_ref[...].astype(F32)
    halo = halo_ref[...].astype(F32)
    s_ref[0:POOL_HALO, :] = jnp.where(t0 == 0, jnp.zeros_like(halo), halo)
    s_ref[POOL_HALO:, :] = u
    tpos = t0 + lax.broadcasted_iota(jnp.int32, (tm, 1), 0)
    parts = []
    for g, win in enumerate(POOL_WINDOWS):
        cs = slice(g * LANES, (g + 1) * LANES)
        ug = u[:, cs]
        ws = ug
        for k in range(1, win):
            ws = ws + s_ref[POOL_HALO - k:POOL_HALO - k + tm, cs]
        count = jnp.minimum(tpos + 1, win).astype(F32)
        y = ws / count - ug
        parts.append((_dot(y.astype(BF16), pw_ref[g]) * ps_ref[:, cs]).astype(BF16))
    cat = jnp.concatenate(parts + [attn_ref[...]], axis=1)
    o_ref[...] = x_ref[...] + _dot(cat, wo_ref[...])


def _even_out(x, proj, attn, pool_w, pool_scale, w_out, seq, *, tm=ROW_TILE):
    m, d = x.shape
    pw = len(POOL_WINDOWS) * LANES
    hb = tm // POOL_HALO
    return pl.pallas_call(
        functools.partial(_even_out_kernel, seq=seq),
        grid=(m // tm,),
        in_specs=[pl.BlockSpec((tm, d), lambda r: (r, 0)),
                  pl.BlockSpec((tm, pw), lambda r: (r, 0)),
                  pl.BlockSpec((POOL_HALO, pw), lambda r: (jnp.maximum(r * hb - 1, 0), 0)),
                  pl.BlockSpec((tm, attn.shape[1]), lambda r: (r, 0)),
                  _const_spec(pool_w.shape), _const_spec(pool_scale.shape), _const_spec(w_out.shape)],
        out_specs=pl.BlockSpec((tm, d), lambda r: (r, 0)),
        out_shape=jax.ShapeDtypeStruct((m, d), F32),
        scratch_shapes=[pltpu.VMEM((tm + POOL_HALO, pw), F32)],
        compiler_params=_params("parallel"),
        name="even_out",
    )(x, proj, proj, attn, pool_w, pool_scale, w_out)


def _ffn_ple_kernel(x_ref, p_ref, fg_ref, wup_ref, cw_ref, wdn_ref, pg_ref, wg_ref, wp_ref, fn_ref, o_ref,
                    acc_ref, ybuf_ref, carry_ref, *, seq, tf, final_norm):
    tm = x_ref.shape[0]
    ffn = wdn_ref.shape[0]
    first = (pl.program_id(0) * tm) % seq == 0
    x = x_ref[...]
    xn = _rms(x, fg_ref[...]).astype(BF16)
    for c in range(ffn // tf):
        conv = []
        for part in range(2):
            cs = slice(part * ffn + c * tf, part * ffn + (c + 1) * tf)
            y = _dot(xn, wup_ref[:, cs])
            prev = carry_ref[:, cs]
            ybuf_ref[part, 0:CONV_HALO, :] = jnp.where(first, jnp.zeros_like(prev), prev)
            ybuf_ref[part, CONV_HALO:, :] = y
            carry_ref[:, cs] = y[tm - CONV_HALO:, :]
            w = cw_ref[:, cs]
            conv.append(w[2:3] * y
                        + w[1:2] * ybuf_ref[part, CONV_HALO - 1:CONV_HALO - 1 + tm, :]
                        + w[0:1] * ybuf_ref[part, CONV_HALO - 2:CONV_HALO - 2 + tm, :])
        h = (conv[0] * jax.nn.sigmoid(conv[0]) * conv[1]).astype(BF16)
        down = _dot(h, wdn_ref[c * tf:(c + 1) * tf, :])
        if c == 0:
            acc_ref[...] = down
        else:
            acc_ref[...] += down
    x2 = x + acc_ref[...]
    gate = jax.nn.sigmoid(_dot(_rms(x2, pg_ref[...]).astype(BF16), wg_ref[...]))
    x3 = x2 + _dot(p_ref[...].astype(BF16), wp_ref[...]) * gate
    if final_norm:
        x3 = _rms(x3, fn_ref[...])
    o_ref[...] = x3


def _ffn_ple(x, p, ffn_gain, w_up, conv_w, w_down, ple_gain, w_gate, w_ple, final_gain, seq, final_norm,
             *, tm=ROW_TILE, tf=FFN_CHUNK):
    m, d = x.shape
    row = lambda r: (r, 0)
    return pl.pallas_call(
        functools.partial(_ffn_ple_kernel, seq=seq, tf=tf, final_norm=final_norm),
        grid=(m // tm,),
        in_specs=[pl.BlockSpec((tm, d), row), pl.BlockSpec((tm, p.shape[1]), row),
                  _const_spec((1, d)), _const_spec(w_up.shape), _const_spec(conv_w.shape),
                  _const_spec(w_down.shape), _const_spec((1, d)), _const_spec(w_gate.shape),
                  _const_spec(w_ple.shape), _const_spec((1, d))],
        out_specs=pl.BlockSpec((tm, d), row),
        out_shape=jax.ShapeDtypeStruct((m, d), F32),
        scratch_shapes=[pltpu.VMEM((tm, d), F32), pltpu.VMEM((2, tm + CONV_HALO, tf), F32),
                        pltpu.VMEM((CONV_HALO, w_up.shape[1]), F32)],
        compiler_params=_params("arbitrary"),
        name="ffn_ple",
    )(x, p, ffn_gain, w_up, conv_w, w_down, ple_gain, w_gate, w_ple, final_gain)


def _gdn_kernel(q_ref, k_ref, v_ref, z_ref, gb_ref, cq_ref, ck_ref, cv_ref, nw_ref, o_ref,
                cbuf_ref, tr_ref, kwq_ref, n_ref, o0_ref, gl_ref, s_ref, *, n_heads, hg):
    seq = q_ref.shape[1]
    tile, chunk, dh = GDN_TILE, GDN_CHUNK, GDN_HEAD_DIM
    cpt = tile // chunk
    row = lax.broadcasted_iota(jnp.int32, (tile, tile), 0)
    col = lax.broadcasted_iota(jnp.int32, (tile, tile), 1)
    same = (row // chunk) == (col // chunk)
    incl = jnp.logical_and(same, row >= col)
    strict = jnp.logical_and(same, row > col)
    lane = lax.broadcasted_iota(jnp.int32, (tile, LANES), 1)

    def conv_silu(src_ref, cw_ref, slot, r0, hl):
        raw = src_ref[0, pl.ds(r0, tile), hl].astype(F32)
        h0 = pl.multiple_of(jnp.maximum(r0 - GDN_HALO, 0), GDN_HALO)
        halo = src_ref[0, pl.ds(h0, GDN_HALO), hl].astype(F32)
        cbuf_ref[slot, 0:GDN_HALO, :] = jnp.where(r0 > 0, halo, jnp.zeros_like(halo))
        cbuf_ref[slot, GDN_HALO:, :] = raw
        w = cw_ref[:, hl]
        y = w[GDN_CONV - 1:GDN_CONV] * raw
        for i in range(1, GDN_CONV):
            y = y + w[GDN_CONV - 1 - i:GDN_CONV - i] * cbuf_ref[slot, GDN_HALO - i:GDN_HALO - i + tile, :]
        return y * jax.nn.sigmoid(y)

    def l2n(a):
        return a * lax.rsqrt(jnp.sum(a * a, axis=-1, keepdims=True) + EPS)

    def block_diag(wide):
        return jnp.where(same, jnp.concatenate([wide] * cpt, axis=0), 0.0).astype(BF16)

    streams = [(dt, h) for dt in range(GDN_TILES_PER_STEP) for h in range(hg)]
    ns = range(len(streams))
    hls = [slice(h * dh, (h + 1) * dh) for _, h in streams]
    heads = [pl.program_id(1) * hg + h for _, h in streams]

    def pre(ti, carry):
        tis = [ti * GDN_TILES_PER_STEP + dt for dt, _ in streams]
        r0s = [pl.multiple_of(t * tile, tile) for t in tis]
        gbts = [gb_ref[0, pl.ds(r0, tile), :] for r0 in r0s]
        for s in ns:
            if streams[s][1] == 0:
                tr_ref[streams[s][0]] = gbts[s].T
        q = [l2n(conv_silu(q_ref, cq_ref, 3 * s, r0s[s], hls[s])) * (dh ** -0.5) for s in ns]
        k = [l2n(conv_silu(k_ref, ck_ref, 3 * s + 1, r0s[s], hls[s])) for s in ns]
        v = [conv_silu(v_ref, cv_ref, 3 * s + 2, r0s[s], hls[s]) for s in ns]
        beta = [jnp.sum(jnp.where(lane == heads[s], gbts[s], 0.0), axis=-1, keepdims=True) for s in ns]
        gc = [jnp.sum(jnp.where(lane == n_heads + heads[s], gbts[s], 0.0), axis=-1, keepdims=True) for s in ns]
        gc_b = [jnp.broadcast_to(gc[s], (tile, dh)) for s in ns]
        gc_last = [jnp.broadcast_to(gc_b[s].reshape(cpt, chunk, dh)[:, chunk - 1:chunk, :],
                                    (cpt, chunk, dh)).reshape(tile, dh) for s in ns]
        gc_row = [tr_ref[streams[s][0], pl.ds(n_heads + heads[s], 1), :] for s in ns]
        decay = [jnp.where(incl, jnp.exp(jnp.where(incl, gc[s] - gc_row[s], 0.0)), 0.0) for s in ns]
        k_beta = [k[s] * beta[s] for s in ns]
        kq = [_dot_nt(jnp.concatenate([k_beta[s].astype(BF16), q[s].astype(BF16)], axis=0), k[s].astype(BF16))
              for s in ns]
        a_bd = [jnp.where(strict, kq[s][:tile] * decay[s], 0.0) for s in ns]
        qk16 = [(kq[s][tile:] * decay[s]).astype(BF16) for s in ns]
        pw = [-(a_bd[s][0:chunk] + a_bd[s][chunk:2 * chunk] + a_bd[s][2 * chunk:3 * chunk] + a_bd[s][3 * chunk:])
              for s in ns]
        res = list(pw)
        pw = [_dot(pw[s].astype(BF16), block_diag(pw[s])) for s in ns]
        step = 2
        while step < chunk:
            both = [_dot(jnp.concatenate([res[s], pw[s]], axis=0).astype(BF16), block_diag(pw[s])) for s in ns]
            res = [res[s] + pw[s] + both[s][:chunk] for s in ns]
            pw = [both[s][chunk:] for s in ns]
            step *= 2
        egc = [jnp.exp(gc_b[s]) for s in ns]
        rhs = [jnp.concatenate([v[s] * beta[s], k_beta[s] * egc[s]], axis=1) for s in ns]
        sol = [rhs[s] + _dot(block_diag(res[s]), rhs[s].astype(BF16)) for s in ns]
        sol16 = [sol[s].astype(BF16) for s in ns]
        qkuw = [_dot(qk16[s], sol16[s]) for s in ns]
        kd16 = [(k[s] * jnp.exp(gc_last[s] - gc_b[s])).astype(BF16) for s in ns]
        for s in ns:
            hl = hls[s]
            o0_ref[pl.ds(r0s[s], tile), hl] = qkuw[s][:, :dh]
            qe16 = (q[s] * egc[s] - qkuw[s][:, dh:]).astype(BF16)
            gl = jnp.exp(gc_last[s])
            for c in range(cpt):
                cr = slice(c * chunk, (c + 1) * chunk)
                nkw = _dot_tn(kd16[s][cr], sol16[s][cr])
                n_ref[tis[s] * cpt + c, :, hl] = nkw[:, :dh]
                kwq_ref[tis[s] * cpt + c, 0:dh, hl] = nkw[:, dh:].astype(BF16)
                kwq_ref[tis[s] * cpt + c, dh:, hl] = qe16[cr]
                gl_ref[tis[s] * cpt + c, :, hl] = gl[c * chunk:c * chunk + 8]
        return carry

    lax.fori_loop(0, seq // (tile * GDN_TILES_PER_STEP), pre, 0)

    s_ref[...] = jnp.zeros_like(s_ref)
    zero = jnp.zeros((dh, dh), BF16)

    def scan(n, carry):
        rows = pl.ds(pl.multiple_of(n * chunk, chunk), chunk)
        state = s_ref[...]
        s16 = state.astype(BF16)
        s_bd = jnp.concatenate([jnp.concatenate([s16[:, :dh], zero], axis=1),
                                jnp.concatenate([zero, s16[:, dh:]], axis=1)], axis=0)
        both = _dot(kwq_ref[n], s_bd)
        s_ref[...] = state * jnp.concatenate([gl_ref[n]] * (dh // 8), axis=0) + n_ref[n] - both[:dh]
        o = both[dh:] + o0_ref[rows, :]
        for h in range(hg):
            hl = slice(h * dh, (h + 1) * dh)
            oh = o[:, hl]
            oh = oh * lax.rsqrt(jnp.mean(oh * oh, axis=-1, keepdims=True) + EPS) * nw_ref[...]
            zg = z_ref[0, rows, hl].astype(F32)
            o_ref[0, rows, hl] = (oh * (zg * jax.nn.sigmoid(zg))).astype(o_ref.dtype)
        return carry

    lax.fori_loop(0, seq // chunk, scan, 0)


def _gdn(proj, gb, conv_w, norm_w, n_batch, seq, n_heads):
    hg = 2
    width = hg * GDN_HEAD_DIM
    groups = n_heads // hg
    n_chunks = seq // GDN_CHUNK
    blk = (1, seq, width)
    col = lambda base: (lambda b, p: (b, 0, base * groups + p))
    ccol = lambda base: (lambda b, p: (0, base * groups + p))
    return pl.pallas_call(
        functools.partial(_gdn_kernel, n_heads=n_heads, hg=hg),
        grid=(n_batch, groups),
        in_specs=[pl.BlockSpec(blk, col(0)), pl.BlockSpec(blk, col(1)), pl.BlockSpec(blk, col(2)),
                  pl.BlockSpec(blk, col(3)),
                  pl.BlockSpec((1, seq, LANES), lambda b, p: (b, 0, 0)),
                  pl.BlockSpec((GDN_CONV, width), ccol(0)), pl.BlockSpec((GDN_CONV, width), ccol(1)),
                  pl.BlockSpec((GDN_CONV, width), ccol(2)),
                  _const_spec((1, GDN_HEAD_DIM))],
        out_specs=pl.BlockSpec(blk, lambda b, p: (b, 0, p)),
        out_shape=jax.ShapeDtypeStruct((n_batch, seq, n_heads * GDN_HEAD_DIM), BF16),
        scratch_shapes=[pltpu.VMEM((3 * hg * GDN_TILES_PER_STEP, GDN_TILE + GDN_HALO, GDN_HEAD_DIM), F32),
                        pltpu.VMEM((GDN_TILES_PER_STEP, LANES, GDN_TILE), F32),
                        pltpu.VMEM((n_chunks, GDN_HEAD_DIM + GDN_CHUNK, width), BF16),
                        pltpu.VMEM((n_chunks, GDN_HEAD_DIM, width), F32),
                        pltpu.VMEM((seq, width), F32),
                        pltpu.VMEM((n_chunks, 8, width), F32),
                        pltpu.VMEM((GDN_HEAD_DIM, width), F32)],
        compiler_params=_params("parallel", "parallel"),
        name="gdn",
    )(proj, proj, proj, proj, gb, conv_w, conv_w, conv_w, norm_w)


def _matmul_residual_kernel(x_ref, a_ref, w_ref, o_ref):
    o_ref[...] = x_ref[...] + _dot(a_ref[...], w_ref[...])


def _matmul_residual(x, a, w, *, tm=ROW_TILE):
    m, d = x.shape
    return pl.pallas_call(
        _matmul_residual_kernel,
        grid=(m // tm,),
        in_specs=[pl.BlockSpec((tm, d), lambda r: (r, 0)), pl.BlockSpec((tm, a.shape[1]), lambda r: (r, 0)),
                  _const_spec(w.shape)],
        out_specs=pl.BlockSpec((tm, d), lambda r: (r, 0)),
        out_shape=jax.ShapeDtypeStruct((m, d), F32),
        compiler_params=_params("parallel"),
        name="matmul_residual",
    )(x, a, w)


def kernel(x, p, mix_norm_e, w_in_e, pool_w, pool_scale, w_out_e, mix_norm_o, w_in_o, conv_qkv_o, a_log_o,
           dt_bias_o, gdn_norm_o, w_out_o, ffn_norm, w_up, ffn_conv, w_down, ple_norm, w_ple_gate, w_ple,
           final_norm):
    n_batch, seq, d = x.shape
    depth = p.shape[0]
    m = n_batch * seq
    pool_width = len(POOL_WINDOWS) * LANES
    row = lambda a: a.reshape(1, -1)
    xf = x.reshape(m, d)
    for i in range(depth):
        j = i // 2
        if i % 2 == 0:
            sb_width = (w_in_e.shape[2] - pool_width) // 3
            scale = jnp.concatenate([jnp.ones((pool_width,), F32),
                                     jnp.full((sb_width,), SB_HEAD_DIM ** -0.5, F32),
                                     jnp.ones((2 * sb_width,), F32)])
            proj = _norm_matmul(xf, row(mix_norm_e[j]), (w_in_e[j] * scale).astype(BF16))
            pc, sc = pool_width // LANES, sb_width // LANES
            attn = _sb_attention(proj.reshape(n_batch, seq, -1), n_batch, seq, pc, pc + sc, pc + 2 * sc, sc)
            xf = _even_out(xf, proj, attn.reshape(m, sb_width), pool_w[j].astype(BF16), row(pool_scale[j]),
                           w_out_e[j].astype(BF16), seq)
        else:
            n_heads = a_log_o.shape[1]
            mix = n_heads * GDN_HEAD_DIM
            wba = jnp.pad(w_in_o[j][:, 4 * mix:], ((0, 0), (0, LANES - 2 * n_heads))).astype(BF16)
            lane_pad = lambda a: jnp.pad(a, (n_heads, LANES - 2 * n_heads)).reshape(1, LANES)
            proj, gb = _odd_in_proj(xf, row(mix_norm_o[j]), w_in_o[j][:, :4 * mix].astype(BF16), wba,
                                    lane_pad(a_log_o[j]), lane_pad(dt_bias_o[j]), n_heads)
            o = _gdn(proj.reshape(n_batch, seq, -1), gb.reshape(n_batch, seq, LANES), conv_qkv_o[j],
                     row(gdn_norm_o[j]), n_batch, seq, n_heads)
            xf = _matmul_residual(xf, o.reshape(m, mix), w_out_o[j].astype(BF16))
        xf = _ffn_ple(xf, p[i].reshape(m, -1), row(ffn_norm[i]), w_up[i].astype(BF16), ffn_conv[i],
                      w_down[i].astype(BF16), row(ple_norm[i]), w_ple_gate[i].astype(BF16),
                      w_ple[i].astype(BF16), row(final_norm), seq, i == depth - 1)
    return xf.reshape(n_batch, seq, d)
```

```python
import functools

import jax
import jax.numpy as jnp
from jax import lax
from jax.experimental import pallas as pl
from jax.experimental.pallas import tpu as pltpu

F32 = jnp.float32
BF16 = jnp.bfloat16

EPS = 1e-6
LANES = 128
POOL_WINDOWS = (2, 4, 8, 16)
POOL_HALO = 16
SB_HEAD_DIM = 64
SB_TILE = 128
SB_LOG_UNDERFLOW = 88.0
GDN_HEAD_DIM = 128
GDN_CHUNK = 64
GDN_TILE = 256
GDN_TILES_PER_STEP = 2
GDN_CONV = 4
GDN_HALO = 16
FFN_CONV = 3
CONV_HALO = 8
ROW_TILE = 512
FFN_CHUNK = 256
VMEM_LIMIT_BYTES = 56 * 1024 * 1024


def _params(*sem):
    return pltpu.CompilerParams(dimension_semantics=sem, vmem_limit_bytes=VMEM_LIMIT_BYTES)


def _const_spec(shape):
    nd = len(shape)
    return pl.BlockSpec(shape, lambda *_: (0,) * nd, pipeline_mode=pl.Buffered(1))


def _rms(x, gain):
    return x * lax.rsqrt(jnp.mean(x * x, axis=-1, keepdims=True) + EPS) * gain


def _dot(a, b):
    return jnp.dot(a, b, preferred_element_type=F32)


def _dot_nt(a, b):
    return lax.dot_general(a, b, (((1,), (1,)), ((), ())), preferred_element_type=F32)


def _dot_tn(a, b):
    return lax.dot_general(a, b, (((0,), (0,)), ((), ())), preferred_element_type=F32)


def _split_bf16(x):
    hi = x.astype(BF16)
    lo = (x - hi.astype(F32)).astype(BF16)
    return hi, lo


def _norm_matmul_kernel(x_ref, g_ref, w_ref, o_ref, *, n_chunk):
    xn = _rms(x_ref[...], g_ref[...]).astype(BF16)
    for c in range(0, w_ref.shape[1], n_chunk):
        o_ref[:, c:c + n_chunk] = _dot(xn, w_ref[:, c:c + n_chunk]).astype(o_ref.dtype)


def _norm_matmul(x, gain, w, *, tm=ROW_TILE, n_chunk=512):
    m, d = x.shape
    n = w.shape[1]
    return pl.pallas_call(
        functools.partial(_norm_matmul_kernel, n_chunk=n_chunk),
        grid=(m // tm,),
        in_specs=[pl.BlockSpec((tm, d), lambda r: (r, 0)), _const_spec((1, d)), _const_spec((d, n))],
        out_specs=pl.BlockSpec((tm, n), lambda r: (r, 0)),
        out_shape=jax.ShapeDtypeStruct((m, n), BF16),
        compiler_params=_params("parallel"),
        name="norm_matmul",
    )(x, gain, w)


def _odd_in_kernel(x_ref, g_ref, w_ref, wba_ref, alog_ref, dt_ref, o_ref, gb_ref, *, n_chunk, n_heads):
    xn = _rms(x_ref[...], g_ref[...]).astype(BF16)
    for c in range(0, w_ref.shape[1], n_chunk):
        o_ref[:, c:c + n_chunk] = _dot(xn, w_ref[:, c:c + n_chunk]).astype(o_ref.dtype)
    ba = _dot(xn, wba_ref[...])
    lane = lax.broadcasted_iota(jnp.int32, ba.shape, 1)
    is_g = jnp.logical_and(lane >= n_heads, lane < 2 * n_heads)
    g = jnp.where(is_g, -jnp.exp(alog_ref[...]) * jax.nn.softplus(ba + dt_ref[...]), 0.0)
    beta = jax.nn.sigmoid(ba)
    sub = GDN_TILE
    row = lax.broadcasted_iota(jnp.int32, (sub, sub), 0)
    col = lax.broadcasted_iota(jnp.int32, (sub, sub), 1)
    tri = jnp.where(jnp.logical_and(row // GDN_CHUNK == col // GDN_CHUNK, row >= col), 1.0, 0.0).astype(BF16)
    tri2 = jnp.concatenate([tri, tri], axis=1)
    is_beta = lax.broadcasted_iota(jnp.int32, (sub, LANES), 1) < n_heads
    for r in range(0, ba.shape[0], sub):
        hi, lo = _split_bf16(g[r:r + sub])
        gc = _dot(tri2, jnp.concatenate([hi, lo], axis=0))
        gb_ref[r:r + sub, :] = jnp.where(is_beta, beta[r:r + sub], gc)


def _odd_in_proj(x, gain, w, wba, alog_lane, dt_lane, n_heads, *, tm=ROW_TILE, n_chunk=512):
    m, d = x.shape
    n = w.shape[1]
    return pl.pallas_call(
        functools.partial(_odd_in_kernel, n_chunk=n_chunk, n_heads=n_heads),
        grid=(m // tm,),
        in_specs=[pl.BlockSpec((tm, d), lambda r: (r, 0)), _const_spec((1, d)), _const_spec((d, n)),
                  _const_spec((d, LANES)), _const_spec((1, LANES)), _const_spec((1, LANES))],
        out_specs=[pl.BlockSpec((tm, n), lambda r: (r, 0)), pl.BlockSpec((tm, LANES), lambda r: (r, 0))],
        out_shape=[jax.ShapeDtypeStruct((m, n), BF16), jax.ShapeDtypeStruct((m, LANES), F32)],
        compiler_params=_params("parallel"),
        name="odd_in_proj",
    )(x, gain, w, wba, alog_lane, dt_lane)


def _sb_attn_kernel(q_ref, k_ref, v_ref, o_ref, acc_ref, c_ref):
    t = SB_TILE
    seq = q_ref.shape[1]
    lane = lax.broadcasted_iota(jnp.int32, (t, LANES), 1)
    head0 = lane < SB_HEAD_DIM
    row = lax.broadcasted_iota(jnp.int32, (t, t), 0)
    col = lax.broadcasted_iota(jnp.int32, (t, t), 1)
    r2 = lax.broadcasted_iota(jnp.int32, (2 * t, 2 * t), 0) % t
    c2 = lax.broadcasted_iota(jnp.int32, (2 * t, 2 * t), 1)
    cum = jnp.where((c2 >= t) | (r2 > c2), 1.0, 0.0).astype(BF16)

    below_diag = col < row

    def rows_of(i):
        return pl.ds(i * t, t) if isinstance(i, int) else pl.ds(pl.multiple_of(i * t, t), t)

    def split_heads(x):
        zero = jnp.zeros_like(x)
        return (jnp.where(head0, x, zero), jnp.where(head0, zero, x))

    def log_sigmoids(qm, kb):
        z = _dot_nt(qm, kb)
        soft = jnp.log(1.0 + jnp.exp(-jnp.abs(z)))
        return jnp.minimum(z, 0.0) - soft, jnp.minimum(-z, 0.0) - soft

    def running(ls_neg):
        hi, lo = _split_bf16(ls_neg)
        return _dot(jnp.concatenate([hi, lo], axis=1), cum)

    def sweep(i, slot, j_start, top_start):
        qh = split_heads(q_ref[0, rows_of(i), :])

        def cond(st):
            j, top = st
            return jnp.logical_and(j >= 0, top > -SB_LOG_UNDERFLOW)

        def body(st):
            j, _ = st
            kb = k_ref[0, rows_of(j), :]
            vh = split_heads(v_ref[0, rows_of(j), :])
            valid = (col - row) < (i - j) * t
            acc = acc_ref[slot]
            top = jnp.float32(-jnp.inf)
            for h in range(2):
                ls_pos, ls_neg = log_sigmoids(qh[h], kb)
                run = running(jnp.where(valid, ls_neg, 0.0))
                c_old = c_ref[slot, h]
                a = jnp.where(valid, jnp.exp(ls_pos + run[:, :t] + c_old), 0.0)
                acc = acc + _dot(a.astype(BF16), vh[h])
                c_new = c_old + run[:, t:]
                c_ref[slot, h] = c_new
                top = jnp.maximum(top, jnp.max(c_new))
            acc_ref[slot] = acc
            return j - 1, top

        lax.while_loop(cond, body, (j_start, top_start))
        o_ref[0, rows_of(i), :] = acc_ref[slot].astype(o_ref.dtype)

    for i in range(2):
        acc_ref[0] = jnp.zeros((t, LANES), F32)
        c_ref[0] = jnp.zeros((2, t, LANES), F32)
        sweep(i, 0, jnp.int32(i), jnp.float32(0.0))

    def tile_pair(ip, carry):
        tiles = (2 * ip, 2 * ip + 1)
        streams = [(a, b, h) for a in range(2) for b in range(2) for h in range(2)]
        qh = [split_heads(q_ref[0, rows_of(i), :]) for i in tiles]
        kb = [[k_ref[0, rows_of(i - b), :] for b in range(2)] for i in tiles]
        vh = [[split_heads(v_ref[0, rows_of(i - b), :]) for b in range(2)] for i in tiles]
        ls = {s: log_sigmoids(qh[s[0]][s[2]], kb[s[0]][s[1]]) for s in streams}
        run = {s: running(jnp.where(below_diag, ls[s][1], 0.0) if s[1] == 0 else ls[s][1]) for s in streams}
        tops = []
        for a in range(2):
            acc = jnp.zeros((t, LANES), F32)
            top = jnp.float32(-jnp.inf)
            for h in range(2):
                d, l = (a, 0, h), (a, 1, h)
                w_diag = jnp.where(below_diag, jnp.exp(ls[d][0] + run[d][:, :t]), 0.0)
                w_left = jnp.exp(ls[l][0] + run[l][:, :t] + run[d][:, t:])
                acc = acc + _dot(jnp.concatenate([w_diag, w_left], axis=1).astype(BF16),
                                 jnp.concatenate([vh[a][0][h], vh[a][1][h]], axis=0))
                c_new = run[d][:, t:] + run[l][:, t:]
                c_ref[a, h] = c_new
                top = jnp.maximum(top, jnp.max(c_new))
            acc_ref[a] = acc
            tops.append(top)
        for a in range(2):
            sweep(tiles[a], a, tiles[a] - 2, tops[a])
        return carry

    lax.fori_loop(1, seq // (2 * t), tile_pair, 0)


def _sb_attention(proj, n_batch, seq, q_col, k_col, v_col, n_pairs):
    blk = (1, seq, LANES)
    return pl.pallas_call(
        _sb_attn_kernel,
        grid=(n_batch, n_pairs),
        in_specs=[pl.BlockSpec(blk, lambda b, p: (b, 0, q_col + p)),
                  pl.BlockSpec(blk, lambda b, p: (b, 0, k_col + p)),
                  pl.BlockSpec(blk, lambda b, p: (b, 0, v_col + p))],
        out_specs=pl.BlockSpec(blk, lambda b, p: (b, 0, p)),
        out_shape=jax.ShapeDtypeStruct((n_batch, seq, n_pairs * LANES), BF16),
        scratch_shapes=[pltpu.VMEM((2, SB_TILE, LANES), F32), pltpu.VMEM((2, 2, SB_TILE, LANES), F32)],
        compiler_params=_params("parallel", "parallel"),
        name="sb_attention",
    )(proj, proj, proj)


def _even_out_kernel(x_ref, u_ref, halo_ref, attn_ref, pw_ref, ps_ref, wo_ref, o_ref, s_ref, *, seq):
    tm = x_ref.shape[0]
    t0 = (pl.program_id(0) * tm) % seq
    u = u_ref[...].astype(F32)
    halo = halo_ref[...].astype(F32)
    s_ref[0:POOL_HALO, :] = jnp.where(t0 == 0, jnp.zeros_like(halo), halo)
    s_ref[POOL_HALO:, :] = u
    tpos = t0 + lax.broadcasted_iota(jnp.int32, (tm, 1), 0)
    parts = []
    for g, win in enumerate(POOL_WINDOWS):
        cs = slice(g * LANES, (g + 1) * LANES)
        ug = u[:, cs]
        ws = ug
        for k in range(1, win):
            ws = ws + s_ref[POOL_HALO - k:POOL_HALO - k + tm, cs]
        count = jnp.minimum(tpos + 1, win).astype(F32)
        y = ws / count - ug
        parts.append((_dot(y.astype(BF16), pw_ref[g]) * ps_ref[:, cs]).astype(BF16))
    cat = jnp.concatenate(parts + [attn_ref[...]], axis=1)
    o_ref[...] = x_ref[...] + _dot(cat, wo_ref[...])


def _even_out(x, proj, attn, pool_w, pool_scale, w_out, seq, *, tm=ROW_TILE):
    m, d = x.shape
    pw = len(POOL_WINDOWS) * LANES
    hb = tm // POOL_HALO
    return pl.pallas_call(
        functools.partial(_even_out_kernel, seq=seq),
        grid=(m // tm,),
        in_specs=[pl.BlockSpec((tm, d), lambda r: (r, 0)),
                  pl.BlockSpec((tm, pw), lambda r: (r, 0)),
                  pl.BlockSpec((POOL_HALO, pw), lambda r: (jnp.maximum(r * hb - 1, 0), 0)),
                  pl.BlockSpec((tm, attn.shape[1]), lambda r: (r, 0)),
                  _const_spec(pool_w.shape), _const_spec(pool_scale.shape), _const_spec(w_out.shape)],
        out_specs=pl.BlockSpec((tm, d), lambda r: (r, 0)),
        out_shape=jax.ShapeDtypeStruct((m, d), F32),
        scratch_shapes=[pltpu.VMEM((tm + POOL_HALO, pw), F32)],
        compiler_params=_params("parallel"),
        name="even_out",
    )(x, proj, proj, attn, pool_w, pool_scale, w_out)


def _ffn_ple_kernel(x_ref, p_ref, fg_ref, wup_ref, cw_ref, wdn_ref, pg_ref, wg_ref, wp_ref, fn_ref, o_ref,
                    h_ref, ybuf_ref, carry_ref, *, seq, tf, final_norm):
    tm = x_ref.shape[0]
    ffn = wdn_ref.shape[0]
    first = (pl.program_id(0) * tm) % seq == 0
    x = x_ref[...]
    xn = _rms(x, fg_ref[...]).astype(BF16)
    for c in range(ffn // tf):
        conv = []
        for part in range(2):
            cs = slice(part * ffn + c * tf, part * ffn + (c + 1) * tf)
            y = _dot(xn, wup_ref[:, cs])
            prev = carry_ref[:, cs]
            ybuf_ref[part, 0:CONV_HALO, :] = jnp.where(first, jnp.zeros_like(prev), prev)
            ybuf_ref[part, CONV_HALO:, :] = y
            carry_ref[:, cs] = y[tm - CONV_HALO:, :]
            w = cw_ref[:, cs]
            conv.append(w[2:3] * y
                        + w[1:2] * ybuf_ref[part, CONV_HALO - 1:CONV_HALO - 1 + tm, :]
                        + w[0:1] * ybuf_ref[part, CONV_HALO - 2:CONV_HALO - 2 + tm, :])
        h_ref[:, c * tf:(c + 1) * tf] = (conv[0] * jax.nn.sigmoid(conv[0]) * conv[1]).astype(BF16)
    x2 = x + _dot(h_ref[...], wdn_ref[...])
    gate = jax.nn.sigmoid(_dot(_rms(x2, pg_ref[...]).astype(BF16), wg_ref[...]))
    x3 = x2 + _dot(p_ref[...].astype(BF16), wp_ref[...]) * gate
    if final_norm:
        x3 = _rms(x3, fn_ref[...])
    o_ref[...] = x3


def _ffn_ple(x, p, ffn_gain, w_up, conv_w, w_down, ple_gain, w_gate, w_ple, final_gain, seq, final_norm,
             *, tm=ROW_TILE, tf=FFN_CHUNK):
    m, d = x.shape
    row = lambda r: (r, 0)
    return pl.pallas_call(
        functools.partial(_ffn_ple_kernel, seq=seq, tf=tf, final_norm=final_norm),
        grid=(m // tm,),
        in_specs=[pl.BlockSpec((tm, d), row), pl.BlockSpec((tm, p.shape[1]), row),
                  _const_spec((1, d)), _const_spec(w_up.shape), _const_spec(conv_w.shape),
                  _const_spec(w_down.shape), _const_spec((1, d)), _const_spec(w_gate.shape),
                  _const_spec(w_ple.shape), _const_spec((1, d))],
        out_specs=pl.BlockSpec((tm, d), row),
        out_shape=jax.ShapeDtypeStruct((m, d), F32),
        scratch_shapes=[pltpu.VMEM((tm, w_down.shape[0]), BF16), pltpu.VMEM((2, tm + CONV_HALO, tf), F32),
                        pltpu.VMEM((CONV_HALO, w_up.shape[1]), F32)],
        compiler_params=_params("arbitrary"),
        name="ffn_ple",
    )(x, p, ffn_gain, w_up, conv_w, w_down, ple_gain, w_gate, w_ple, final_gain)


def _gdn_kernel(q_ref, k_ref, v_ref, z_ref, gb_ref, cq_ref, ck_ref, cv_ref, nw_ref, o_ref,
                cbuf_ref, tr_ref, kwq_ref, n_ref, o0_ref, gl_ref, s_ref, *, n_heads, hg):
    seq = q_ref.shape[1]
    tile, chunk, dh = GDN_TILE, GDN_CHUNK, GDN_HEAD_DIM
    cpt = tile // chunk
    row = lax.broadcasted_iota(jnp.int32, (tile, tile), 0)
    col = lax.broadcasted_iota(jnp.int32, (tile, tile), 1)
    same = (row // chunk) == (col // chunk)
    incl = jnp.logical_and(same, row >= col)
    strict = jnp.logical_and(same, row > col)
    lane = lax.broadcasted_iota(jnp.int32, (tile, LANES), 1)

    def conv_silu(src_ref, cw_ref, slot, r0, hl):
        raw = src_ref[0, pl.ds(r0, tile), hl].astype(F32)
        h0 = pl.multiple_of(jnp.maximum(r0 - GDN_HALO, 0), GDN_HALO)
        halo = src_ref[0, pl.ds(h0, GDN_HALO), hl].astype(F32)
        cbuf_ref[slot, 0:GDN_HALO, :] = jnp.where(r0 > 0, halo, jnp.zeros_like(halo))
        cbuf_ref[slot, GDN_HALO:, :] = raw
        w = cw_ref[:, hl]
        y = w[GDN_CONV - 1:GDN_CONV] * raw
        for i in range(1, GDN_CONV):
            y = y + w[GDN_CONV - 1 - i:GDN_CONV - i] * cbuf_ref[slot, GDN_HALO - i:GDN_HALO - i + tile, :]
        return y * jax.nn.sigmoid(y)

    def l2n(a):
        return a * lax.rsqrt(jnp.sum(a * a, axis=-1, keepdims=True) + EPS)

    def block_diag(wide):
        return jnp.where(same, jnp.concatenate([wide] * cpt, axis=0), 0.0).astype(BF16)

    streams = [(dt, h) for dt in range(GDN_TILES_PER_STEP) for h in range(hg)]
    ns = range(len(streams))
    hls = [slice(h * dh, (h + 1) * dh) for _, h in streams]
    heads = [pl.program_id(1) * hg + h for _, h in streams]

    def pre(ti, carry):
        tis = [ti * GDN_TILES_PER_STEP + dt for dt, _ in streams]
        r0s = [pl.multiple_of(t * tile, tile) for t in tis]
        gbts = [gb_ref[0, pl.ds(r0, tile), :] for r0 in r0s]
        for s in ns:
            if streams[s][1] == 0:
                tr_ref[streams[s][0]] = gbts[s].T
        q = [l2n(conv_silu(q_ref, cq_ref, 3 * s, r0s[s], hls[s])) * (dh ** -0.5) for s in ns]
        k = [l2n(conv_silu(k_ref, ck_ref, 3 * s + 1, r0s[s], hls[s])) for s in ns]
        v = [conv_silu(v_ref, cv_ref, 3 * s + 2, r0s[s], hls[s]) for s in ns]
        beta = [jnp.sum(jnp.where(lane == heads[s], gbts[s], 0.0), axis=-1, keepdims=True) for s in ns]
        gc = [jnp.sum(jnp.where(lane == n_heads + heads[s], gbts[s], 0.0), axis=-1, keepdims=True) for s in ns]
        gc_b = [jnp.broadcast_to(gc[s], (tile, dh)) for s in ns]
        gc_last = [jnp.broadcast_to(gc_b[s].reshape(cpt, chunk, dh)[:, chunk - 1:chunk, :],
                                    (cpt, chunk, dh)).reshape(tile, dh) for s in ns]
        gc_row = [tr_ref[streams[s][0], pl.ds(n_heads + heads[s], 1), :] for s in ns]
        decay = [jnp.where(incl, jnp.exp(jnp.where(incl, gc[s] - gc_row[s], 0.0)), 0.0) for s in ns]
        k_beta = [k[s] * beta[s] for s in ns]
        kq = [_dot_nt(jnp.concatenate([k_beta[s].astype(BF16), q[s].astype(BF16)], axis=0), k[s].astype(BF16))
              for s in ns]
        a_bd = [jnp.where(strict, kq[s][:tile] * decay[s], 0.0) for s in ns]
        qk16 = [(kq[s][tile:] * decay[s]).astype(BF16) for s in ns]
        pw = [-(a_bd[s][0:chunk] + a_bd[s][chunk:2 * chunk] + a_bd[s][2 * chunk:3 * chunk] + a_bd[s][3 * chunk:])
              for s in ns]
        res = list(pw)
        pw = [_dot(pw[s].astype(BF16), block_diag(pw[s])) for s in ns]
        step = 2
        while step < chunk:
            both = [_dot(jnp.concatenate([res[s], pw[s]], axis=0).astype(BF16), block_diag(pw[s])) for s in ns]
            res = [res[s] + pw[s] + both[s][:chunk] for s in ns]
            pw = [both[s][chunk:] for s in ns]
            step *= 2
        egc = [jnp.exp(gc_b[s]) for s in ns]
        rhs = [jnp.concatenate([v[s] * beta[s], k_beta[s] * egc[s]], axis=1) for s in ns]
        sol = [rhs[s] + _dot(block_diag(res[s]), rhs[s].astype(BF16)) for s in ns]
        sol16 = [sol[s].astype(BF16) for s in ns]
        qkuw = [_dot(qk16[s], sol16[s]) for s in ns]
        kd16 = [(k[s] * jnp.exp(gc_last[s] - gc_b[s])).astype(BF16) for s in ns]
        for s in ns:
            hl = hls[s]
            o0_ref[pl.ds(r0s[s], tile), hl] = qkuw[s][:, :dh]
            qe16 = (q[s] * egc[s] - qkuw[s][:, dh:]).astype(BF16)
            gl = jnp.exp(gc_last[s])
            for c in range(cpt):
                cr = slice(c * chunk, (c + 1) * chunk)
                nkw = _dot_tn(kd16[s][cr], sol16[s][cr])
                n_ref[tis[s] * cpt + c, :, hl] = nkw[:, :dh]
                kwq_ref[tis[s] * cpt + c, 0:dh, hl] = nkw[:, dh:].astype(BF16)
                kwq_ref[tis[s] * cpt + c, dh:, hl] = qe16[cr]
                gl_ref[tis[s] * cpt + c, :, hl] = gl[c * chunk:c * chunk + 8]
        return carry

    lax.fori_loop(0, seq // (tile * GDN_TILES_PER_STEP), pre, 0)

    s_ref[...] = jnp.zeros_like(s_ref)
    zero = jnp.zeros((dh, dh), BF16)

    def scan(n, carry):
        rows = pl.ds(pl.multiple_of(n * chunk, chunk), chunk)
        state = s_ref[...]
        s16 = state.astype(BF16)
        s_bd = jnp.concatenate([jnp.concatenate([s16[:, :dh], zero], axis=1),
                                jnp.concatenate([zero, s16[:, dh:]], axis=1)], axis=0)
        both = _dot(kwq_ref[n], s_bd)
        s_ref[...] = state * jnp.concatenate([gl_ref[n]] * (dh // 8), axis=0) + n_ref[n] - both[:dh]
        o = both[dh:] + o0_ref[rows, :]
        for h in range(hg):
            hl = slice(h * dh, (h + 1) * dh)
            oh = o[:, hl]
            oh = oh * lax.rsqrt(jnp.mean(oh * oh, axis=-1, keepdims=True) + EPS) * nw_ref[...]
            zg = z_ref[0, rows, hl].astype(F32)
            o_ref[0, rows, hl] = (oh * (zg * jax.nn.sigmoid(zg))).astype(o_ref.dtype)
        return carry

    lax.fori_loop(0, seq // chunk, scan, 0)


def _gdn(proj, gb, conv_w, norm_w, n_batch, seq, n_heads):
    hg = 2
    width = hg * GDN_HEAD_DIM
    groups = n_heads // hg
    n_chunks = seq // GDN_CHUNK
    blk = (1, seq, width)
    col = lambda base: (lambda b, p: (b, 0, base * groups + p))
    ccol = lambda base: (lambda b, p: (0, base * groups + p))
    return pl.pallas_call(
        functools.partial(_gdn_kernel, n_heads=n_heads, hg=hg),
        grid=(n_batch, groups),
        in_specs=[pl.BlockSpec(blk, col(0)), pl.BlockSpec(blk, col(1)), pl.BlockSpec(blk, col(2)),
                  pl.BlockSpec(blk, col(3)),
                  pl.BlockSpec((1, seq, LANES), lambda b, p: (b, 0, 0)),
                  pl.BlockSpec((GDN_CONV, width), ccol(0)), pl.BlockSpec((GDN_CONV, width), ccol(1)),
                  pl.BlockSpec((GDN_CONV, width), ccol(2)),
                  _const_spec((1, GDN_HEAD_DIM))],
        out_specs=pl.BlockSpec(blk, lambda b, p: (b, 0, p)),
        out_shape=jax.ShapeDtypeStruct((n_batch, seq, n_heads * GDN_HEAD_DIM), BF16),
        scratch_shapes=[pltpu.VMEM((3 * hg * GDN_TILES_PER_STEP, GDN_TILE + GDN_HALO, GDN_HEAD_DIM), F32),
                        pltpu.VMEM((GDN_TILES_PER_STEP, LANES, GDN_TILE), F32),
                        pltpu.VMEM((n_chunks, GDN_HEAD_DIM + GDN_CHUNK, width), BF16),
                        pltpu.VMEM((n_chunks, GDN_HEAD_DIM, width), F32),
                        pltpu.VMEM((seq, width), F32),
                        pltpu.VMEM((n_chunks, 8, width), F32),
                        pltpu.VMEM((GDN_HEAD_DIM, width), F32)],
        compiler_params=_params("parallel", "parallel"),
        name="gdn",
    )(proj, proj, proj, proj, gb, conv_w, conv_w, conv_w, norm_w)


def _matmul_residual_kernel(x_ref, a_ref, w_ref, o_ref):
    o_ref[...] = x_ref[...] + _dot(a_ref[...], w_ref[...])


def _matmul_residual(x, a, w, *, tm=ROW_TILE):
    m, d = x.shape
    return pl.pallas_call(
        _matmul_residual_kernel,
        grid=(m // tm,),
        in_specs=[pl.BlockSpec((tm, d), lambda r: (r, 0)), pl.BlockSpec((tm, a.shape[1]), lambda r: (r, 0)),
                  _const_spec(w.shape)],
        out_specs=pl.BlockSpec((tm, d), lambda r: (r, 0)),
        out_shape=jax.ShapeDtypeStruct((m, d), F32),
        compiler_params=_params("parallel"),
        name="matmul_residual",
    )(x, a, w)


def kernel(x, p, mix_norm_e, w_in_e, pool_w, pool_scale, w_out_e, mix_norm_o, w_in_o, conv_qkv_o, a_log_o,
           dt_bias_o, gdn_norm_o, w_out_o, ffn_norm, w_up, ffn_conv, w_down, ple_norm, w_ple_gate, w_ple,
           final_norm):
    n_batch, seq, d = x.shape
    depth = p.shape[0]
    m = n_batch * seq
    pool_width = len(POOL_WINDOWS) * LANES
    row = lambda a: a.reshape(1, -1)
    xf = x.reshape(m, d)
    for i in range(depth):
        j = i // 2
        if i % 2 == 0:
            sb_width = (w_in_e.shape[2] - pool_width) // 3
            scale = jnp.concatenate([jnp.ones((pool_width,), F32),
                                     jnp.full((sb_width,), SB_HEAD_DIM ** -0.5, F32),
                                     jnp.ones((2 * sb_width,), F32)])
            proj = _norm_matmul(xf, row(mix_norm_e[j]), (w_in_e[j] * scale).astype(BF16))
            pc, sc = pool_width // LANES, sb_width // LANES
            attn = _sb_attention(proj.reshape(n_batch, seq, -1), n_batch, seq, pc, pc + sc, pc + 2 * sc, sc)
            xf = _even_out(xf, proj, attn.reshape(m, sb_width), pool_w[j].astype(BF16), row(pool_scale[j]),
                           w_out_e[j].astype(BF16), seq)
        else:
            n_heads = a_log_o.shape[1]
            mix = n_heads * GDN_HEAD_DIM
            wba = jnp.pad(w_in_o[j][:, 4 * mix:], ((0, 0), (0, LANES - 2 * n_heads))).astype(BF16)
            lane_pad = lambda a: jnp.pad(a, (n_heads, LANES - 2 * n_heads)).reshape(1, LANES)
            proj, gb = _odd_in_proj(xf, row(mix_norm_o[j]), w_in_o[j][:, :4 * mix].astype(BF16), wba,
                                    lane_pad(a_log_o[j]), lane_pad(dt_bias_o[j]), n_heads)
            o = _gdn(proj.reshape(n_batch, seq, -1), gb.reshape(n_batch, seq, LANES), conv_qkv_o[j],
                     row(gdn_norm_o[j]), n_batch, seq, n_heads)
            xf = _matmul_residual(xf, o.reshape(m, mix), w_out_o[j].astype(BF16))
        xf = _ffn_ple(xf, p[i].reshape(m, -1), row(ffn_norm[i]), w_up[i].astype(BF16), ffn_conv[i],
                      w_down[i].astype(BF16), row(ple_norm[i]), w_ple_gate[i].astype(BF16),
                      w_ple[i].astype(BF16), row(final_norm), seq, i == depth - 1)
    return xf.reshape(n_batch, seq, d)
```

```python
import functools

import jax
import jax.numpy as jnp
from jax import lax
from jax.experimental import pallas as pl
from jax.experimental.pallas import tpu as pltpu

F32 = jnp.float32
BF16 = jnp.bfloat16

EPS = 1e-6
LANES = 128
POOL_WINDOWS = (2, 4, 8, 16)
POOL_HALO = 16
SB_HEAD_DIM = 64
SB_TILE = 128
SB_LOG_UNDERFLOW = 88.0
GDN_HEAD_DIM = 128
GDN_CHUNK = 64
GDN_TILE = 256
GDN_TILES_PER_STEP = 2
GDN_CONV = 4
GDN_HALO = 16
FFN_CONV = 3
CONV_HALO = 8
ROW_TILE = 512
FFN_CHUNK = 256
VMEM_LIMIT_BYTES = 56 * 1024 * 1024


def _params(*sem):
    return pltpu.CompilerParams(dimension_semantics=sem, vmem_limit_bytes=VMEM_LIMIT_BYTES)


def _const_spec(shape):
    nd = len(shape)
    return pl.BlockSpec(shape, lambda *_: (0,) * nd, pipeline_mode=pl.Buffered(1))


def _rms(x, gain):
    return x * lax.rsqrt(jnp.mean(x * x, axis=-1, keepdims=True) + EPS) * gain


def _dot(a, b):
    return jnp.dot(a, b, preferred_element_type=F32)


def _dot_nt(a, b):
    return lax.dot_general(a, b, (((1,), (1,)), ((), ())), preferred_element_type=F32)


def _dot_tn(a, b):
    return lax.dot_general(a, b, (((0,), (0,)), ((), ())), preferred_element_type=F32)


def _split_bf16(x):
    hi = x.astype(BF16)
    lo = (x - hi.astype(F32)).astype(BF16)
    return hi, lo


def _norm_matmul_kernel(x_ref, g_ref, w_ref, o_ref, *, n_chunk):
    xn = _rms(x_ref[...], g_ref[...]).astype(BF16)
    for c in range(0, w_ref.shape[1], n_chunk):
        o_ref[:, c:c + n_chunk] = _dot(xn, w_ref[:, c:c + n_chunk]).astype(o_ref.dtype)


def _norm_matmul(x, gain, w, *, tm=ROW_TILE, n_chunk=512):
    m, d = x.shape
    n = w.shape[1]
    return pl.pallas_call(
        functools.partial(_norm_matmul_kernel, n_chunk=n_chunk),
        grid=(m // tm,),
        in_specs=[pl.BlockSpec((tm, d), lambda r: (r, 0)), _const_spec((1, d)), _const_spec((d, n))],
        out_specs=pl.BlockSpec((tm, n), lambda r: (r, 0)),
        out_shape=jax.ShapeDtypeStruct((m, n), BF16),
        compiler_params=_params("parallel"),
        name="norm_matmul",
    )(x, gain, w)


def _odd_in_kernel(x_ref, g_ref, w_ref, cw_ref, wba_ref, alog_ref, dt_ref, o_ref, gb_ref, ybuf_ref, carry_ref,
                   *, n_chunk, n_heads, seq):
    tm = x_ref.shape[0]
    mix = n_heads * GDN_HEAD_DIM
    first = (pl.program_id(0) * tm) % seq == 0
    xn = _rms(x_ref[...], g_ref[...]).astype(BF16)
    for c in range(0, w_ref.shape[1], n_chunk):
        cs = slice(c, c + n_chunk)
        y = _dot(xn, w_ref[:, cs])
        if c < 3 * mix:
            prev = carry_ref[:, cs]
            ybuf_ref[0:CONV_HALO, :] = jnp.where(first, jnp.zeros_like(prev), prev)
            ybuf_ref[CONV_HALO:, :] = y
            carry_ref[:, cs] = y[tm - CONV_HALO:, :]
            w = cw_ref[:, cs]
            y = w[GDN_CONV - 1:GDN_CONV] * y
            for i in range(1, GDN_CONV):
                y = y + w[GDN_CONV - 1 - i:GDN_CONV - i] * ybuf_ref[CONV_HALO - i:CONV_HALO - i + tm, :]
            y = y * jax.nn.sigmoid(y)
            if c < 2 * mix:
                scale = GDN_HEAD_DIM ** -0.5 if c < mix else 1.0
                heads = []
                for hc in range(0, n_chunk, GDN_HEAD_DIM):
                    yh = y[:, hc:hc + GDN_HEAD_DIM]
                    inv = lax.rsqrt(jnp.sum(yh * yh, axis=-1, keepdims=True) + EPS)
                    heads.append(yh * (inv * scale))
                y = jnp.concatenate(heads, axis=1)
        o_ref[:, cs] = y.astype(o_ref.dtype)
    ba = _dot(xn, wba_ref[...])
    lane = lax.broadcasted_iota(jnp.int32, ba.shape, 1)
    is_g = jnp.logical_and(lane >= n_heads, lane < 2 * n_heads)
    g = jnp.where(is_g, -jnp.exp(alog_ref[...]) * jax.nn.softplus(ba + dt_ref[...]), 0.0)
    beta = jax.nn.sigmoid(ba)
    sub = GDN_TILE
    row = lax.broadcasted_iota(jnp.int32, (sub, sub), 0)
    col = lax.broadcasted_iota(jnp.int32, (sub, sub), 1)
    tri = jnp.where(jnp.logical_and(row // GDN_CHUNK == col // GDN_CHUNK, row >= col), 1.0, 0.0).astype(BF16)
    tri2 = jnp.concatenate([tri, tri], axis=1)
    is_beta = lax.broadcasted_iota(jnp.int32, (sub, LANES), 1) < n_heads
    for r in range(0, ba.shape[0], sub):
        hi, lo = _split_bf16(g[r:r + sub])
        gc = _dot(tri2, jnp.concatenate([hi, lo], axis=0))
        gb_ref[r:r + sub, :] = jnp.where(is_beta, beta[r:r + sub], gc)


def _odd_in_proj(x, gain, w, conv_w, wba, alog_lane, dt_lane, n_heads, seq, *, tm=ROW_TILE, n_chunk=512):
    m, d = x.shape
    n = w.shape[1]
    return pl.pallas_call(
        functools.partial(_odd_in_kernel, n_chunk=n_chunk, n_heads=n_heads, seq=seq),
        grid=(m // tm,),
        in_specs=[pl.BlockSpec((tm, d), lambda r: (r, 0)), _const_spec((1, d)), _const_spec((d, n)),
                  _const_spec(conv_w.shape), _const_spec((d, LANES)), _const_spec((1, LANES)),
                  _const_spec((1, LANES))],
        out_specs=[pl.BlockSpec((tm, n), lambda r: (r, 0)), pl.BlockSpec((tm, LANES), lambda r: (r, 0))],
        out_shape=[jax.ShapeDtypeStruct((m, n), BF16), jax.ShapeDtypeStruct((m, LANES), F32)],
        scratch_shapes=[pltpu.VMEM((tm + CONV_HALO, n_chunk), F32), pltpu.VMEM((CONV_HALO, conv_w.shape[1]), F32)],
        compiler_params=_params("arbitrary"),
        name="odd_in_proj",
    )(x, gain, w, conv_w, wba, alog_lane, dt_lane)


def _sb_attn_kernel(q_ref, k_ref, v_ref, o_ref, acc_ref, c_ref):
    t = SB_TILE
    seq = q_ref.shape[1]
    lane = lax.broadcasted_iota(jnp.int32, (t, LANES), 1)
    head0 = lane < SB_HEAD_DIM
    row = lax.broadcasted_iota(jnp.int32, (t, t), 0)
    col = lax.broadcasted_iota(jnp.int32, (t, t), 1)
    r2 = lax.broadcasted_iota(jnp.int32, (2 * t, 2 * t), 0) % t
    c2 = lax.broadcasted_iota(jnp.int32, (2 * t, 2 * t), 1)
    cum = jnp.where((c2 >= t) | (r2 > c2), 1.0, 0.0).astype(BF16)

    below_diag = col < row

    def rows_of(i):
        return pl.ds(i * t, t) if isinstance(i, int) else pl.ds(pl.multiple_of(i * t, t), t)

    def split_heads(x):
        zero = jnp.zeros_like(x)
        return (jnp.where(head0, x, zero), jnp.where(head0, zero, x))

    def log_sigmoids(qm, kb):
        z = _dot_nt(qm, kb)
        soft = jnp.log(1.0 + jnp.exp(-jnp.abs(z)))
        return jnp.minimum(z, 0.0) - soft, jnp.minimum(-z, 0.0) - soft

    def running(ls_neg):
        hi, lo = _split_bf16(ls_neg)
        return _dot(jnp.concatenate([hi, lo], axis=1), cum)

    def sweep(i, slot, j_start, top_start):
        qh = split_heads(q_ref[0, rows_of(i), :])

        def cond(st):
            j, top = st
            return jnp.logical_and(j >= 0, top > -SB_LOG_UNDERFLOW)

        def body(st):
            j, _ = st
            kb = k_ref[0, rows_of(j), :]
            vh = split_heads(v_ref[0, rows_of(j), :])
            valid = (col - row) < (i - j) * t
            acc = acc_ref[slot]
            top = jnp.float32(-jnp.inf)
            for h in range(2):
                ls_pos, ls_neg = log_sigmoids(qh[h], kb)
                run = running(jnp.where(valid, ls_neg, 0.0))
                c_old = c_ref[slot, h]
                a = jnp.where(valid, jnp.exp(ls_pos + run[:, :t] + c_old), 0.0)
                acc = acc + _dot(a.astype(BF16), vh[h])
                c_new = c_old + run[:, t:]
                c_ref[slot, h] = c_new
                top = jnp.maximum(top, jnp.max(c_new))
            acc_ref[slot] = acc
            return j - 1, top

        lax.while_loop(cond, body, (j_start, top_start))
        o_ref[0, rows_of(i), :] = acc_ref[slot].astype(o_ref.dtype)

    for i in range(2):
        acc_ref[0] = jnp.zeros((t, LANES), F32)
        c_ref[0] = jnp.zeros((2, t, LANES), F32)
        sweep(i, 0, jnp.int32(i), jnp.float32(0.0))

    def tile_pair(ip, carry):
        tiles = (2 * ip, 2 * ip + 1)
        streams = [(a, b, h) for a in range(2) for b in range(2) for h in range(2)]
        qh = [split_heads(q_ref[0, rows_of(i), :]) for i in tiles]
        kb = [[k_ref[0, rows_of(i - b), :] for b in range(2)] for i in tiles]
        vh = [[split_heads(v_ref[0, rows_of(i - b), :]) for b in range(2)] for i in tiles]
        ls = {s: log_sigmoids(qh[s[0]][s[2]], kb[s[0]][s[1]]) for s in streams}
        run = {s: running(jnp.where(below_diag, ls[s][1], 0.0) if s[1] == 0 else ls[s][1]) for s in streams}
        tops = []
        for a in range(2):
            acc = jnp.zeros((t, LANES), F32)
            top = jnp.float32(-jnp.inf)
            for h in range(2):
                d, l = (a, 0, h), (a, 1, h)
                w_diag = jnp.where(below_diag, jnp.exp(ls[d][0] + run[d][:, :t]), 0.0)
                w_left = jnp.exp(ls[l][0] + run[l][:, :t] + run[d][:, t:])
                acc = acc + _dot(jnp.concatenate([w_diag, w_left], axis=1).astype(BF16),
                                 jnp.concatenate([vh[a][0][h], vh[a][1][h]], axis=0))
                c_new = run[d][:, t:] + run[l][:, t:]
                c_ref[a, h] = c_new
                top = jnp.maximum(top, jnp.max(c_new))
            acc_ref[a] = acc
            tops.append(top)
        for a in range(2):
            sweep(tiles[a], a, tiles[a] - 2, tops[a])
        return carry

    lax.fori_loop(1, seq // (2 * t), tile_pair, 0)


def _sb_attention(proj, n_batch, seq, q_col, k_col, v_col, n_pairs):
    blk = (1, seq, LANES)
    return pl.pallas_call(
        _sb_attn_kernel,
        grid=(n_batch, n_pairs),
        in_specs=[pl.BlockSpec(blk, lambda b, p: (b, 0, q_col + p)),
                  pl.BlockSpec(blk, lambda b, p: (b, 0, k_col + p)),
                  pl.BlockSpec(blk, lambda b, p: (b, 0, v_col + p))],
        out_specs=pl.BlockSpec(blk, lambda b, p: (b, 0, p)),
        out_shape=jax.ShapeDtypeStruct((n_batch, seq, n_pairs * LANES), BF16),
        scratch_shapes=[pltpu.VMEM((2, SB_TILE, LANES), F32), pltpu.VMEM((2, 2, SB_TILE, LANES), F32)],
        compiler_params=_params("parallel", "parallel"),
        name="sb_attention",
    )(proj, proj, proj)


def _even_out_kernel(x_ref, u_ref, halo_ref, attn_ref, pw_ref, ps_ref, wo_ref, o_ref, s_ref, *, seq):
    tm = x_ref.shape[0]
    t0 = (pl.program_id(0) * tm) % seq
    u = u_ref[...].astype(F32)
    halo = halo_ref[...].astype(F32)
    s_ref[0:POOL_HALO, :] = jnp.where(t0 == 0, jnp.zeros_like(halo), halo)
    s_ref[POOL_HALO:, :] = u
    tpos = t0 + lax.broadcasted_iota(jnp.int32, (tm, 1), 0)
    parts = []
    for g, win in enumerate(POOL_WINDOWS):
        cs = slice(g * LANES, (g + 1) * LANES)
        ug = u[:, cs]
        ws = ug
        for k in range(1, win):
            ws = ws + s_ref[POOL_HALO - k:POOL_HALO - k + tm, cs]
        count = jnp.minimum(tpos + 1, win).astype(F32)
        y = ws / count - ug
        parts.append((_dot(y.astype(BF16), pw_ref[g]) * ps_ref[:, cs]).astype(BF16))
    cat = jnp.concatenate(parts + [attn_ref[...]], axis=1)
    o_ref[...] = x_ref[...] + _dot(cat, wo_ref[...])


def _even_out(x, proj, attn, pool_w, pool_scale, w_out, seq, *, tm=ROW_TILE):
    m, d = x.shape
    pw = len(POOL_WINDOWS) * LANES
    hb = tm // POOL_HALO
    return pl.pallas_call(
        functools.partial(_even_out_kernel, seq=seq),
        grid=(m // tm,),
        in_specs=[pl.BlockSpec((tm, d), lambda r: (r, 0)),
                  pl.BlockSpec((tm, pw), lambda r: (r, 0)),
                  pl.BlockSpec((POOL_HALO, pw), lambda r: (jnp.maximum(r * hb - 1, 0), 0)),
                  pl.BlockSpec((tm, attn.shape[1]), lambda r: (r, 0)),
                  _const_spec(pool_w.shape), _const_spec(pool_scale.shape), _const_spec(w_out.shape)],
        out_specs=pl.BlockSpec((tm, d), lambda r: (r, 0)),
        out_shape=jax.ShapeDtypeStruct((m, d), F32),
        scratch_shapes=[pltpu.VMEM((tm + POOL_HALO, pw), F32)],
        compiler_params=_params("parallel"),
        name="even_out",
    )(x, proj, proj, attn, pool_w, pool_scale, w_out)


def _ffn_ple_kernel(x_ref, p_ref, fg_ref, wup_ref, cw_ref, wdn_ref, pg_ref, wg_ref, wp_ref, fn_ref, o_ref,
                    h_ref, ybuf_ref, carry_ref, *, seq, tf, final_norm):
    tm = x_ref.shape[0]
    ffn = wdn_ref.shape[0]
    first = (pl.program_id(0) * tm) % seq == 0
    x = x_ref[...]
    xn = _rms(x, fg_ref[...]).astype(BF16)
    for c in range(ffn // tf):
        conv = []
        for part in range(2):
            cs = slice(part * ffn + c * tf, part * ffn + (c + 1) * tf)
            y = _dot(xn, wup_ref[:, cs])
            prev = carry_ref[:, cs]
            ybuf_ref[part, 0:CONV_HALO, :] = jnp.where(first, jnp.zeros_like(prev), prev)
            ybuf_ref[part, CONV_HALO:, :] = y
            carry_ref[:, cs] = y[tm - CONV_HALO:, :]
            w = cw_ref[:, cs]
            conv.append(w[2:3] * y
                        + w[1:2] * ybuf_ref[part, CONV_HALO - 1:CONV_HALO - 1 + tm, :]
                        + w[0:1] * ybuf_ref[part, CONV_HALO - 2:CONV_HALO - 2 + tm, :])
        h_ref[:, c * tf:(c + 1) * tf] = (conv[0] * jax.nn.sigmoid(conv[0]) * conv[1]).astype(BF16)
    x2 = x + _dot(h_ref[...], wdn_ref[...])
    gate = jax.nn.sigmoid(_dot(_rms(x2, pg_ref[...]).astype(BF16), wg_ref[...]))
    x3 = x2 + _dot(p_ref[...].astype(BF16), wp_ref[...]) * gate
    if final_norm:
        x3 = _rms(x3, fn_ref[...])
    o_ref[...] = x3


def _ffn_ple(x, p, ffn_gain, w_up, conv_w, w_down, ple_gain, w_gate, w_ple, final_gain, seq, final_norm,
             *, tm=ROW_TILE, tf=FFN_CHUNK):
    m, d = x.shape
    row = lambda r: (r, 0)
    return pl.pallas_call(
        functools.partial(_ffn_ple_kernel, seq=seq, tf=tf, final_norm=final_norm),
        grid=(m // tm,),
        in_specs=[pl.BlockSpec((tm, d), row), pl.BlockSpec((tm, p.shape[1]), row),
                  _const_spec((1, d)), _const_spec(w_up.shape), _const_spec(conv_w.shape),
                  _const_spec(w_down.shape), _const_spec((1, d)), _const_spec(w_gate.shape),
                  _const_spec(w_ple.shape), _const_spec((1, d))],
        out_specs=pl.BlockSpec((tm, d), row),
        out_shape=jax.ShapeDtypeStruct((m, d), F32),
        scratch_shapes=[pltpu.VMEM((tm, w_down.shape[0]), BF16), pltpu.VMEM((2, tm + CONV_HALO, tf), F32),
                        pltpu.VMEM((CONV_HALO, w_up.shape[1]), F32)],
        compiler_params=_params("arbitrary"),
        name="ffn_ple",
    )(x, p, ffn_gain, w_up, conv_w, w_down, ple_gain, w_gate, w_ple, final_gain)


def _gdn_ops_kernel(q_ref, k_ref, v_ref, gb_ref, kwq_ref, n_ref, o0_ref, gl_ref, tr_ref, *, n_heads, hg):
    tile, chunk, dh = GDN_TILE, GDN_CHUNK, GDN_HEAD_DIM
    cpt = tile // chunk
    row = lax.broadcasted_iota(jnp.int32, (tile, tile), 0)
    col = lax.broadcasted_iota(jnp.int32, (tile, tile), 1)
    same = (row // chunk) == (col // chunk)
    incl = jnp.logical_and(same, row >= col)
    strict = jnp.logical_and(same, row > col)
    lane = lax.broadcasted_iota(jnp.int32, (tile, LANES), 1)

    def block_diag(wide):
        return jnp.where(same, jnp.concatenate([wide] * cpt, axis=0), 0.0).astype(BF16)

    streams = [(dt, h) for dt in range(GDN_TILES_PER_STEP) for h in range(hg)]
    ns = range(len(streams))
    hls = [slice(h * dh, (h + 1) * dh) for _, h in streams]
    heads = [pl.program_id(1) * hg + h for _, h in streams]

    r0s = [dt * tile for dt, _ in streams]
    gbts = [gb_ref[0, r0:r0 + tile, :] for r0 in r0s]
    for s in ns:
        if streams[s][1] == 0:
            tr_ref[streams[s][0]] = gbts[s].T
    q = [q_ref[0, r0s[s]:r0s[s] + tile, hls[s]].astype(F32) for s in ns]
    k16 = [k_ref[0, r0s[s]:r0s[s] + tile, hls[s]] for s in ns]
    k = [k16[s].astype(F32) for s in ns]
    v = [v_ref[0, r0s[s]:r0s[s] + tile, hls[s]].astype(F32) for s in ns]
    beta = [jnp.sum(jnp.where(lane == heads[s], gbts[s], 0.0), axis=-1, keepdims=True) for s in ns]
    gc = [jnp.sum(jnp.where(lane == n_heads + heads[s], gbts[s], 0.0), axis=-1, keepdims=True) for s in ns]
    gc_b = [jnp.broadcast_to(gc[s], (tile, dh)) for s in ns]
    gc_last = [jnp.broadcast_to(gc_b[s].reshape(cpt, chunk, dh)[:, chunk - 1:chunk, :],
                                (cpt, chunk, dh)).reshape(tile, dh) for s in ns]
    gc_row = [tr_ref[streams[s][0], pl.ds(n_heads + heads[s], 1), :] for s in ns]
    decay = [jnp.where(incl, jnp.exp(jnp.where(incl, gc[s] - gc_row[s], 0.0)), 0.0) for s in ns]
    k_beta = [k[s] * beta[s] for s in ns]
    kq = [_dot_nt(jnp.concatenate([k_beta[s].astype(BF16), q[s].astype(BF16)], axis=0), k16[s]) for s in ns]
    a_bd = [jnp.where(strict, kq[s][:tile] * decay[s], 0.0) for s in ns]
    qk16 = [(kq[s][tile:] * decay[s]).astype(BF16) for s in ns]
    pw = [-(a_bd[s][0:chunk] + a_bd[s][chunk:2 * chunk] + a_bd[s][2 * chunk:3 * chunk] + a_bd[s][3 * chunk:])
          for s in ns]
    res = list(pw)
    pw = [_dot(pw[s].astype(BF16), block_diag(pw[s])) for s in ns]
    step = 2
    while step < chunk:
        both = [_dot(jnp.concatenate([res[s], pw[s]], axis=0).astype(BF16), block_diag(pw[s])) for s in ns]
        res = [res[s] + pw[s] + both[s][:chunk] for s in ns]
        pw = [both[s][chunk:] for s in ns]
        step *= 2
    egc = [jnp.exp(gc_b[s]) for s in ns]
    rhs = [jnp.concatenate([v[s] * beta[s], k_beta[s] * egc[s]], axis=1) for s in ns]
    sol = [rhs[s] + _dot(block_diag(res[s]), rhs[s].astype(BF16)) for s in ns]
    sol16 = [sol[s].astype(BF16) for s in ns]
    qkuw = [_dot(qk16[s], sol16[s]) for s in ns]
    kd16 = [(k[s] * jnp.exp(gc_last[s] - gc_b[s])).astype(BF16) for s in ns]
    for s in ns:
        hl = hls[s]
        o0_ref[0, r0s[s]:r0s[s] + tile, hl] = qkuw[s][:, :dh].astype(o0_ref.dtype)
        qe16 = (q[s] * egc[s] - qkuw[s][:, dh:]).astype(BF16)
        gl = jnp.exp(gc_last[s])
        for c in range(cpt):
            cr = slice(c * chunk, (c + 1) * chunk)
            ci = streams[s][0] * cpt + c
            nkw = _dot_tn(kd16[s][cr], sol16[s][cr])
            n_ref[0, 0, ci, :, hl] = nkw[:, :dh]
            kwq_ref[0, 0, ci, 0:dh, hl] = nkw[:, dh:].astype(BF16)
            kwq_ref[0, 0, ci, dh:, hl] = qe16[cr]
            gl_ref[0, 0, ci, :, hl] = gl[c * chunk:c * chunk + 8]


def _gdn_scan_kernel(kwq_ref, n_ref, o0_ref, gl_ref, z_ref, nw_ref, o_ref, s_ref, *, hg):
    chunk, dh = GDN_CHUNK, GDN_HEAD_DIM
    groups = kwq_ref.shape[1]
    n_chunks = kwq_ref.shape[2]
    width = hg * dh

    @pl.when(pl.program_id(1) == 0)
    def _():
        s_ref[...] = jnp.zeros_like(s_ref)

    zero = jnp.zeros((dh, dh), BF16)

    def scan(n, carry):
        rows = pl.ds(pl.multiple_of(n * chunk, chunk), chunk)
        state = [s_ref[p] for p in range(groups)]
        s16 = [st.astype(BF16) for st in state]
        s_bd = [jnp.concatenate([jnp.concatenate([x[:, :dh], zero], axis=1),
                                 jnp.concatenate([zero, x[:, dh:]], axis=1)], axis=0) for x in s16]
        both = [_dot(kwq_ref[0, p, n], s_bd[p]) for p in range(groups)]
        for p in range(groups):
            s_ref[p] = (state[p] * jnp.concatenate([gl_ref[0, p, n]] * (dh // 8), axis=0)
                        + n_ref[0, p, n] - both[p][:dh])
        for p in range(groups):
            o = both[p][dh:] + o0_ref[0, rows, p * width:(p + 1) * width].astype(F32)
            for h in range(hg):
                hl = slice(p * width + h * dh, p * width + (h + 1) * dh)
                oh = o[:, h * dh:(h + 1) * dh]
                oh = oh * lax.rsqrt(jnp.mean(oh * oh, axis=-1, keepdims=True) + EPS) * nw_ref[...]
                zg = z_ref[0, rows, hl].astype(F32)
                o_ref[0, rows, hl] = (oh * (zg * jax.nn.sigmoid(zg))).astype(o_ref.dtype)
        return carry

    lax.fori_loop(0, n_chunks, scan, 0)


def _gdn(proj, gb, norm_w, n_batch, seq, n_heads):
    hg = 2
    dh, chunk = GDN_HEAD_DIM, GDN_CHUNK
    width = hg * dh
    groups = n_heads // hg
    mix = n_heads * dh
    rows = GDN_TILE * GDN_TILES_PER_STEP
    cps = rows // chunk
    n_chunks = seq // chunk
    col = lambda base: (lambda b, p, t: (b, t, base * groups + p))
    per_chunk = lambda r: pl.BlockSpec((1, 1, cps, r, width), lambda b, p, t: (b, p, t, 0, 0))
    kwq, n_op, o0, gl = pl.pallas_call(
        functools.partial(_gdn_ops_kernel, n_heads=n_heads, hg=hg),
        grid=(n_batch, groups, seq // rows),
        in_specs=[pl.BlockSpec((1, rows, width), col(0)), pl.BlockSpec((1, rows, width), col(1)),
                  pl.BlockSpec((1, rows, width), col(2)),
                  pl.BlockSpec((1, rows, LANES), lambda b, p, t: (b, t, 0))],
        out_specs=[per_chunk(dh + chunk), per_chunk(dh),
                   pl.BlockSpec((1, rows, width), lambda b, p, t: (b, t, p)), per_chunk(8)],
        out_shape=[jax.ShapeDtypeStruct((n_batch, groups, n_chunks, dh + chunk, width), BF16),
                   jax.ShapeDtypeStruct((n_batch, groups, n_chunks, dh, width), F32),
                   jax.ShapeDtypeStruct((n_batch, seq, mix), BF16),
                   jax.ShapeDtypeStruct((n_batch, groups, n_chunks, 8, width), F32)],
        scratch_shapes=[pltpu.VMEM((GDN_TILES_PER_STEP, LANES, GDN_TILE), F32)],
        compiler_params=_params("parallel", "parallel", "parallel"),
        name="gdn_ops",
    )(proj, proj, proj, gb)
    all_pairs = lambda r: pl.BlockSpec((1, groups, cps, r, width), lambda b, t: (b, 0, t, 0, 0))
    return pl.pallas_call(
        functools.partial(_gdn_scan_kernel, hg=hg),
        grid=(n_batch, seq // rows),
        in_specs=[all_pairs(dh + chunk), all_pairs(dh),
                  pl.BlockSpec((1, rows, mix), lambda b, t: (b, t, 0)), all_pairs(8),
                  pl.BlockSpec((1, rows, mix), lambda b, t: (b, t, 3)),
                  _const_spec((1, dh))],
        out_specs=pl.BlockSpec((1, rows, mix), lambda b, t: (b, t, 0)),
        out_shape=jax.ShapeDtypeStruct((n_batch, seq, mix), BF16),
        scratch_shapes=[pltpu.VMEM((groups, dh, width), F32)],
        compiler_params=_params("parallel", "arbitrary"),
        name="gdn_scan",
    )(kwq, n_op, o0, gl, proj, norm_w)


def _matmul_residual_kernel(x_ref, a_ref, w_ref, o_ref):
    o_ref[...] = x_ref[...] + _dot(a_ref[...], w_ref[...])


def _matmul_residual(x, a, w, *, tm=ROW_TILE):
    m, d = x.shape
    return pl.pallas_call(
        _matmul_residual_kernel,
        grid=(m // tm,),
        in_specs=[pl.BlockSpec((tm, d), lambda r: (r, 0)), pl.BlockSpec((tm, a.shape[1]), lambda r: (r, 0)),
                  _const_spec(w.shape)],
        out_specs=pl.BlockSpec((tm, d), lambda r: (r, 0)),
        out_shape=jax.ShapeDtypeStruct((m, d), F32),
        compiler_params=_params("parallel"),
        name="matmul_residual",
    )(x, a, w)


def kernel(x, p, mix_norm_e, w_in_e, pool_w, pool_scale, w_out_e, mix_norm_o, w_in_o, conv_qkv_o, a_log_o,
           dt_bias_o, gdn_norm_o, w_out_o, ffn_norm, w_up, ffn_conv, w_down, ple_norm, w_ple_gate, w_ple,
           final_norm):
    n_batch, seq, d = x.shape
    depth = p.shape[0]
    m = n_batch * seq
    pool_width = len(POOL_WINDOWS) * LANES
    row = lambda a: a.reshape(1, -1)
    xf = x.reshape(m, d)
    for i in range(depth):
        j = i // 2
        if i % 2 == 0:
            sb_width = (w_in_e.shape[2] - pool_width) // 3
            scale = jnp.concatenate([jnp.ones((pool_width,), F32),
                                     jnp.full((sb_width,), SB_HEAD_DIM ** -0.5, F32),
                                     jnp.ones((2 * sb_width,), F32)])
            proj = _norm_matmul(xf, row(mix_norm_e[j]), (w_in_e[j] * scale).astype(BF16))
            pc, sc = pool_width // LANES, sb_width // LANES
            attn = _sb_attention(proj.reshape(n_batch, seq, -1), n_batch, seq, pc, pc + sc, pc + 2 * sc, sc)
            xf = _even_out(xf, proj, attn.reshape(m, sb_width), pool_w[j].astype(BF16), row(pool_scale[j]),
                           w_out_e[j].astype(BF16), seq)
        else:
            n_heads = a_log_o.shape[1]
            mix = n_heads * GDN_HEAD_DIM
            wba = jnp.pad(w_in_o[j][:, 4 * mix:], ((0, 0), (0, LANES - 2 * n_heads))).astype(BF16)
            lane_pad = lambda a: jnp.pad(a, (n_heads, LANES - 2 * n_heads)).reshape(1, LANES)
            proj, gb = _odd_in_proj(xf, row(mix_norm_o[j]), w_in_o[j][:, :4 * mix].astype(BF16), conv_qkv_o[j],
                                    wba, lane_pad(a_log_o[j]), lane_pad(dt_bias_o[j]), n_heads, seq)
            o = _gdn(proj.reshape(n_batch, seq, -1), gb.reshape(n_batch, seq, LANES), row(gdn_norm_o[j]),
                     n_batch, seq, n_heads)
            xf = _matmul_residual(xf, o.reshape(m, mix), w_out_o[j].astype(BF16))
        xf = _ffn_ple(xf, p[i].reshape(m, -1), row(ffn_norm[i]), w_up[i].astype(BF16), ffn_conv[i],
                      w_down[i].astype(BF16), row(ple_norm[i]), w_ple_gate[i].astype(BF16),
                      w_ple[i].astype(BF16), row(final_norm), seq, i == depth - 1)
    return xf.reshape(n_batch, seq, d)
```

```python
import functools

import jax
import jax.numpy as jnp
from jax import lax
from jax.experimental import pallas as pl
from jax.experimental.pallas import tpu as pltpu

F32 = jnp.float32
BF16 = jnp.bfloat16

EPS = 1e-6
LANES = 128
POOL_WINDOWS = (2, 4, 8, 16)
POOL_HALO = 16
SB_HEAD_DIM = 64
SB_TILE = 128
SB_LOG_UNDERFLOW = 88.0
GDN_HEAD_DIM = 128
GDN_CHUNK = 64
GDN_TILE = 256
GDN_TILES_PER_STEP = 4
GDN_CONV = 4
GDN_HALO = 16
FFN_CONV = 3
CONV_HALO = 8
ROW_TILE = 512
FFN_CHUNK = 256
VMEM_LIMIT_BYTES = 56 * 1024 * 1024


def _params(*sem):
    return pltpu.CompilerParams(dimension_semantics=sem, vmem_limit_bytes=VMEM_LIMIT_BYTES)


def _const_spec(shape):
    nd = len(shape)
    return pl.BlockSpec(shape, lambda *_: (0,) * nd, pipeline_mode=pl.Buffered(1))


def _rms(x, gain):
    return x * lax.rsqrt(jnp.mean(x * x, axis=-1, keepdims=True) + EPS) * gain


def _dot(a, b):
    return jnp.dot(a, b, preferred_element_type=F32)


def _dot_nt(a, b):
    return lax.dot_general(a, b, (((1,), (1,)), ((), ())), preferred_element_type=F32)


def _dot_tn(a, b):
    return lax.dot_general(a, b, (((0,), (0,)), ((), ())), preferred_element_type=F32)


def _split_bf16(x):
    hi = x.astype(BF16)
    lo = (x - hi.astype(F32)).astype(BF16)
    return hi, lo


def _norm_matmul_kernel(x_ref, g_ref, w_ref, o_ref, *, n_chunk):
    xn = _rms(x_ref[...], g_ref[...]).astype(BF16)
    for c in range(0, w_ref.shape[1], n_chunk):
        o_ref[:, c:c + n_chunk] = _dot(xn, w_ref[:, c:c + n_chunk]).astype(o_ref.dtype)


def _norm_matmul(x, gain, w, *, tm=ROW_TILE, n_chunk=512):
    m, d = x.shape
    n = w.shape[1]
    return pl.pallas_call(
        functools.partial(_norm_matmul_kernel, n_chunk=n_chunk),
        grid=(m // tm,),
        in_specs=[pl.BlockSpec((tm, d), lambda r: (r, 0)), _const_spec((1, d)), _const_spec((d, n))],
        out_specs=pl.BlockSpec((tm, n), lambda r: (r, 0)),
        out_shape=jax.ShapeDtypeStruct((m, n), BF16),
        compiler_params=_params("parallel"),
        name="norm_matmul",
    )(x, gain, w)


def _odd_in_kernel(x_ref, g_ref, w_ref, cw_ref, wba_ref, alog_ref, dt_ref, o_ref, gb_ref, ybuf_ref, carry_ref,
                   *, n_chunk, n_heads, seq):
    tm = x_ref.shape[0]
    mix = n_heads * GDN_HEAD_DIM
    first = (pl.program_id(0) * tm) % seq == 0
    xn = _rms(x_ref[...], g_ref[...]).astype(BF16)
    for c in range(0, w_ref.shape[1], n_chunk):
        cs = slice(c, c + n_chunk)
        y = _dot(xn, w_ref[:, cs])
        if c < 3 * mix:
            prev = carry_ref[:, cs]
            ybuf_ref[0:CONV_HALO, :] = jnp.where(first, jnp.zeros_like(prev), prev)
            ybuf_ref[CONV_HALO:, :] = y
            carry_ref[:, cs] = y[tm - CONV_HALO:, :]
            w = cw_ref[:, cs]
            y = w[GDN_CONV - 1:GDN_CONV] * y
            for i in range(1, GDN_CONV):
                y = y + w[GDN_CONV - 1 - i:GDN_CONV - i] * ybuf_ref[CONV_HALO - i:CONV_HALO - i + tm, :]
            y = y * jax.nn.sigmoid(y)
            if c < 2 * mix:
                scale = GDN_HEAD_DIM ** -0.5 if c < mix else 1.0
                heads = []
                for hc in range(0, n_chunk, GDN_HEAD_DIM):
                    yh = y[:, hc:hc + GDN_HEAD_DIM]
                    inv = lax.rsqrt(jnp.sum(yh * yh, axis=-1, keepdims=True) + EPS)
                    heads.append(yh * (inv * scale))
                y = jnp.concatenate(heads, axis=1)
        o_ref[:, cs] = y.astype(o_ref.dtype)
    ba = _dot(xn, wba_ref[...])
    lane = lax.broadcasted_iota(jnp.int32, ba.shape, 1)
    is_g = jnp.logical_and(lane >= n_heads, lane < 2 * n_heads)
    g = jnp.where(is_g, -jnp.exp(alog_ref[...]) * jax.nn.softplus(ba + dt_ref[...]), 0.0)
    beta = jax.nn.sigmoid(ba)
    sub = GDN_TILE
    row = lax.broadcasted_iota(jnp.int32, (sub, sub), 0)
    col = lax.broadcasted_iota(jnp.int32, (sub, sub), 1)
    tri = jnp.where(jnp.logical_and(row // GDN_CHUNK == col // GDN_CHUNK, row >= col), 1.0, 0.0).astype(BF16)
    tri2 = jnp.concatenate([tri, tri], axis=1)
    is_beta = lax.broadcasted_iota(jnp.int32, (sub, LANES), 1) < n_heads
    for r in range(0, ba.shape[0], sub):
        hi, lo = _split_bf16(g[r:r + sub])
        gc = _dot(tri2, jnp.concatenate([hi, lo], axis=0))
        gb_ref[r:r + sub, :] = jnp.where(is_beta, beta[r:r + sub], gc)


def _odd_in_proj(x, gain, w, conv_w, wba, alog_lane, dt_lane, n_heads, seq, *, tm=ROW_TILE, n_chunk=512):
    m, d = x.shape
    n = w.shape[1]
    return pl.pallas_call(
        functools.partial(_odd_in_kernel, n_chunk=n_chunk, n_heads=n_heads, seq=seq),
        grid=(m // tm,),
        in_specs=[pl.BlockSpec((tm, d), lambda r: (r, 0)), _const_spec((1, d)), _const_spec((d, n)),
                  _const_spec(conv_w.shape), _const_spec((d, LANES)), _const_spec((1, LANES)),
                  _const_spec((1, LANES))],
        out_specs=[pl.BlockSpec((tm, n), lambda r: (r, 0)), pl.BlockSpec((tm, LANES), lambda r: (r, 0))],
        out_shape=[jax.ShapeDtypeStruct((m, n), BF16), jax.ShapeDtypeStruct((m, LANES), F32)],
        scratch_shapes=[pltpu.VMEM((tm + CONV_HALO, n_chunk), F32), pltpu.VMEM((CONV_HALO, conv_w.shape[1]), F32)],
        compiler_params=_params("arbitrary"),
        name="odd_in_proj",
    )(x, gain, w, conv_w, wba, alog_lane, dt_lane)


def _sb_attn_kernel(q_ref, k_ref, v_ref, o_ref, acc_ref, c_ref):
    t = SB_TILE
    seq = q_ref.shape[1]
    lane = lax.broadcasted_iota(jnp.int32, (t, LANES), 1)
    head0 = lane < SB_HEAD_DIM
    row = lax.broadcasted_iota(jnp.int32, (t, t), 0)
    col = lax.broadcasted_iota(jnp.int32, (t, t), 1)
    r2 = lax.broadcasted_iota(jnp.int32, (2 * t, 2 * t), 0) % t
    c2 = lax.broadcasted_iota(jnp.int32, (2 * t, 2 * t), 1)
    cum = jnp.where((c2 >= t) | (r2 > c2), 1.0, 0.0).astype(BF16)

    below_diag = col < row

    def rows_of(i):
        return pl.ds(i * t, t) if isinstance(i, int) else pl.ds(pl.multiple_of(i * t, t), t)

    def split_heads(x):
        zero = jnp.zeros_like(x)
        return (jnp.where(head0, x, zero), jnp.where(head0, zero, x))

    def log_sigmoids(qm, kb):
        z = _dot_nt(qm, kb)
        soft = jnp.log(1.0 + jnp.exp(-jnp.abs(z)))
        return jnp.minimum(z, 0.0) - soft, jnp.minimum(-z, 0.0) - soft

    def running(ls_neg):
        hi, lo = _split_bf16(ls_neg)
        return _dot(jnp.concatenate([hi, lo], axis=1), cum)

    def sweep(i, slot, j_start, top_start):
        qh = split_heads(q_ref[0, rows_of(i), :])

        def cond(st):
            j, top = st
            return jnp.logical_and(j >= 0, top > -SB_LOG_UNDERFLOW)

        def body(st):
            j, _ = st
            kb = k_ref[0, rows_of(j), :]
            vh = split_heads(v_ref[0, rows_of(j), :])
            valid = (col - row) < (i - j) * t
            acc = acc_ref[slot]
            top = jnp.float32(-jnp.inf)
            for h in range(2):
                ls_pos, ls_neg = log_sigmoids(qh[h], kb)
                run = running(jnp.where(valid, ls_neg, 0.0))
                c_old = c_ref[slot, h]
                a = jnp.where(valid, jnp.exp(ls_pos + run[:, :t] + c_old), 0.0)
                acc = acc + _dot(a.astype(BF16), vh[h])
                c_new = c_old + run[:, t:]
                c_ref[slot, h] = c_new
                top = jnp.maximum(top, jnp.max(c_new))
            acc_ref[slot] = acc
            return j - 1, top

        lax.while_loop(cond, body, (j_start, top_start))
        o_ref[0, rows_of(i), :] = acc_ref[slot].astype(o_ref.dtype)

    for i in range(2):
        acc_ref[0] = jnp.zeros((t, LANES), F32)
        c_ref[0] = jnp.zeros((2, t, LANES), F32)
        sweep(i, 0, jnp.int32(i), jnp.float32(0.0))

    def tile_pair(ip, carry):
        tiles = (2 * ip, 2 * ip + 1)
        streams = [(a, b, h) for a in range(2) for b in range(2) for h in range(2)]
        qh = [split_heads(q_ref[0, rows_of(i), :]) for i in tiles]
        kb = [[k_ref[0, rows_of(i - b), :] for b in range(2)] for i in tiles]
        vh = [[split_heads(v_ref[0, rows_of(i - b), :]) for b in range(2)] for i in tiles]
        ls = {s: log_sigmoids(qh[s[0]][s[2]], kb[s[0]][s[1]]) for s in streams}
        run = {s: running(jnp.where(below_diag, ls[s][1], 0.0) if s[1] == 0 else ls[s][1]) for s in streams}
        tops = []
        for a in range(2):
            acc = jnp.zeros((t, LANES), F32)
            top = jnp.float32(-jnp.inf)
            for h in range(2):
                d, l = (a, 0, h), (a, 1, h)
                w_diag = jnp.where(below_diag, jnp.exp(ls[d][0] + run[d][:, :t]), 0.0)
                w_left = jnp.exp(ls[l][0] + run[l][:, :t] + run[d][:, t:])
                acc = acc + _dot(jnp.concatenate([w_diag, w_left], axis=1).astype(BF16),
                                 jnp.concatenate([vh[a][0][h], vh[a][1][h]], axis=0))
                c_new = run[d][:, t:] + run[l][:, t:]
                c_ref[a, h] = c_new
                top = jnp.maximum(top, jnp.max(c_new))
            acc_ref[a] = acc
            tops.append(top)
        for a in range(2):
            sweep(tiles[a], a, tiles[a] - 2, tops[a])
        return carry

    lax.fori_loop(1, seq // (2 * t), tile_pair, 0)


def _sb_attention(proj, n_batch, seq, q_col, k_col, v_col, n_pairs):
    blk = (1, seq, LANES)
    return pl.pallas_call(
        _sb_attn_kernel,
        grid=(n_batch, n_pairs),
        in_specs=[pl.BlockSpec(blk, lambda b, p: (b, 0, q_col + p)),
                  pl.BlockSpec(blk, lambda b, p: (b, 0, k_col + p)),
                  pl.BlockSpec(blk, lambda b, p: (b, 0, v_col + p))],
        out_specs=pl.BlockSpec(blk, lambda b, p: (b, 0, p)),
        out_shape=jax.ShapeDtypeStruct((n_batch, seq, n_pairs * LANES), BF16),
        scratch_shapes=[pltpu.VMEM((2, SB_TILE, LANES), F32), pltpu.VMEM((2, 2, SB_TILE, LANES), F32)],
        compiler_params=_params("parallel", "parallel"),
        name="sb_attention",
    )(proj, proj, proj)


def _pool_mixer_out(u_ref, halo_ref, attn_ref, pw_ref, ps_ref, wo_ref, s_ref, seq):
    tm = u_ref.shape[0]
    t0 = (pl.program_id(0) * tm) % seq
    u = u_ref[...].astype(F32)
    halo = halo_ref[...].astype(F32)
    s_ref[0:POOL_HALO, :] = jnp.where(t0 == 0, jnp.zeros_like(halo), halo)
    s_ref[POOL_HALO:, :] = u
    tpos = t0 + lax.broadcasted_iota(jnp.int32, (tm, 1), 0)
    parts = []
    for g, win in enumerate(POOL_WINDOWS):
        cs = slice(g * LANES, (g + 1) * LANES)
        ug = u[:, cs]
        ws = ug
        for k in range(1, win):
            ws = ws + s_ref[POOL_HALO - k:POOL_HALO - k + tm, cs]
        count = jnp.minimum(tpos + 1, win).astype(F32)
        y = ws / count - ug
        parts.append((_dot(y.astype(BF16), pw_ref[g]) * ps_ref[:, cs]).astype(BF16))
    cat = jnp.concatenate(parts + [attn_ref[...]], axis=1)
    return _dot(cat, wo_ref[...])


def _ffn_ple_tail(x, p_ref, fg_ref, wup_ref, cw_ref, wdn_ref, pg_ref, wg_ref, wp_ref, fn_ref, o_ref,
                  h_ref, ybuf_ref, carry_ref, *, seq, tf, final_norm):
    tm = x.shape[0]
    ffn = wdn_ref.shape[0]
    first = (pl.program_id(0) * tm) % seq == 0
    xn = _rms(x, fg_ref[...]).astype(BF16)
    for c in range(ffn // tf):
        conv = []
        for part in range(2):
            cs = slice(part * ffn + c * tf, part * ffn + (c + 1) * tf)
            y = _dot(xn, wup_ref[:, cs])
            prev = carry_ref[:, cs]
            ybuf_ref[part, 0:CONV_HALO, :] = jnp.where(first, jnp.zeros_like(prev), prev)
            ybuf_ref[part, CONV_HALO:, :] = y
            carry_ref[:, cs] = y[tm - CONV_HALO:, :]
            w = cw_ref[:, cs]
            conv.append(w[2:3] * y
                        + w[1:2] * ybuf_ref[part, CONV_HALO - 1:CONV_HALO - 1 + tm, :]
                        + w[0:1] * ybuf_ref[part, CONV_HALO - 2:CONV_HALO - 2 + tm, :])
        h_ref[:, c * tf:(c + 1) * tf] = (conv[0] * jax.nn.sigmoid(conv[0]) * conv[1]).astype(BF16)
    x2 = x + _dot(h_ref[...], wdn_ref[...])
    gate = jax.nn.sigmoid(_dot(_rms(x2, pg_ref[...]).astype(BF16), wg_ref[...]))
    x3 = x2 + _dot(p_ref[...].astype(BF16), wp_ref[...]) * gate
    if final_norm:
        x3 = _rms(x3, fn_ref[...])
    o_ref[...] = x3


def _even_tail_kernel(x_ref, u_ref, halo_ref, attn_ref, pw_ref, ps_ref, wo_ref, *rest, seq, tf, final_norm):
    *ffn_refs, s_ref, h_ref, ybuf_ref, carry_ref = rest
    x = x_ref[...] + _pool_mixer_out(u_ref, halo_ref, attn_ref, pw_ref, ps_ref, wo_ref, s_ref, seq)
    _ffn_ple_tail(x, *ffn_refs, h_ref, ybuf_ref, carry_ref, seq=seq, tf=tf, final_norm=final_norm)


def _odd_tail_kernel(x_ref, a_ref, wo_ref, *rest, seq, tf, final_norm):
    x = x_ref[...] + _dot(a_ref[...], wo_ref[...])
    _ffn_ple_tail(x, *rest, seq=seq, tf=tf, final_norm=final_norm)


def _layer_tail(x, mixer, p, ffn_gain, w_up, conv_w, w_down, ple_gain, w_gate, w_ple, final_gain, seq, final_norm,
                *, tm=ROW_TILE, tf=FFN_CHUNK):
    m, d = x.shape
    row = lambda r: (r, 0)
    scratch = [pltpu.VMEM((tm, w_down.shape[0]), BF16), pltpu.VMEM((2, tm + CONV_HALO, tf), F32),
               pltpu.VMEM((CONV_HALO, w_up.shape[1]), F32)]
    if mixer[0] == "even":
        _, proj, attn, pool_w, pool_scale, w_out = mixer
        pw = len(POOL_WINDOWS) * LANES
        hb = tm // POOL_HALO
        body = _even_tail_kernel
        mixer_args = (proj, proj, attn, pool_w, pool_scale, w_out)
        mixer_specs = [pl.BlockSpec((tm, pw), row),
                       pl.BlockSpec((POOL_HALO, pw), lambda r: (jnp.maximum(r * hb - 1, 0), 0)),
                       pl.BlockSpec((tm, attn.shape[1]), row),
                       _const_spec(pool_w.shape), _const_spec(pool_scale.shape), _const_spec(w_out.shape)]
        scratch = [pltpu.VMEM((tm + POOL_HALO, pw), F32)] + scratch
    else:
        _, o, w_out = mixer
        body = _odd_tail_kernel
        mixer_args = (o, w_out)
        mixer_specs = [pl.BlockSpec((tm, o.shape[1]), row), _const_spec(w_out.shape)]
    return pl.pallas_call(
        functools.partial(body, seq=seq, tf=tf, final_norm=final_norm),
        grid=(m // tm,),
        in_specs=[pl.BlockSpec((tm, d), row)] + mixer_specs
                 + [pl.BlockSpec((tm, p.shape[1]), row),
                    _const_spec((1, d)), _const_spec(w_up.shape), _const_spec(conv_w.shape),
                    _const_spec(w_down.shape), _const_spec((1, d)), _const_spec(w_gate.shape),
                    _const_spec(w_ple.shape), _const_spec((1, d))],
        out_specs=pl.BlockSpec((tm, d), row),
        out_shape=jax.ShapeDtypeStruct((m, d), F32),
        scratch_shapes=scratch,
        compiler_params=_params("arbitrary"),
        name=mixer[0] + "_tail",
    )(x, *mixer_args, p, ffn_gain, w_up, conv_w, w_down, ple_gain, w_gate, w_ple, final_gain)


def _gdn_ops_kernel(q_ref, k_ref, v_ref, gb_ref, kwq_ref, n_ref, o0_ref, gl_ref, tr_ref, *, n_heads, hg):
    tile, chunk, dh = GDN_TILE, GDN_CHUNK, GDN_HEAD_DIM
    cpt = tile // chunk
    row = lax.broadcasted_iota(jnp.int32, (tile, tile), 0)
    col = lax.broadcasted_iota(jnp.int32, (tile, tile), 1)
    same = (row // chunk) == (col // chunk)
    incl = jnp.logical_and(same, row >= col)
    strict = jnp.logical_and(same, row > col)
    lane = lax.broadcasted_iota(jnp.int32, (tile, LANES), 1)

    def block_diag(wide):
        return jnp.where(same, jnp.concatenate([wide] * cpt, axis=0), 0.0).astype(BF16)

    streams = [(dt, h) for dt in range(GDN_TILES_PER_STEP) for h in range(hg)]
    ns = range(len(streams))
    hls = [slice(h * dh, (h + 1) * dh) for _, h in streams]
    heads = [pl.program_id(1) * hg + h for _, h in streams]

    r0s = [dt * tile for dt, _ in streams]
    gbts = [gb_ref[0, r0:r0 + tile, :] for r0 in r0s]
    for s in ns:
        if streams[s][1] == 0:
            tr_ref[streams[s][0]] = gbts[s].T
    q = [q_ref[0, r0s[s]:r0s[s] + tile, hls[s]].astype(F32) for s in ns]
    k16 = [k_ref[0, r0s[s]:r0s[s] + tile, hls[s]] for s in ns]
    k = [k16[s].astype(F32) for s in ns]
    v = [v_ref[0, r0s[s]:r0s[s] + tile, hls[s]].astype(F32) for s in ns]
    beta = [jnp.sum(jnp.where(lane == heads[s], gbts[s], 0.0), axis=-1, keepdims=True) for s in ns]
    gc = [jnp.sum(jnp.where(lane == n_heads + heads[s], gbts[s], 0.0), axis=-1, keepdims=True) for s in ns]
    gc_b = [jnp.broadcast_to(gc[s], (tile, dh)) for s in ns]
    gc_last = [jnp.broadcast_to(gc_b[s].reshape(cpt, chunk, dh)[:, chunk - 1:chunk, :],
                                (cpt, chunk, dh)).reshape(tile, dh) for s in ns]
    gc_row = [tr_ref[streams[s][0], pl.ds(n_heads + heads[s], 1), :] for s in ns]
    decay = [jnp.where(incl, jnp.exp(jnp.where(incl, gc[s] - gc_row[s], 0.0)), 0.0) for s in ns]
    k_beta = [k[s] * beta[s] for s in ns]
    kq = [_dot_nt(jnp.concatenate([k_beta[s].astype(BF16), q[s].astype(BF16)], axis=0), k16[s]) for s in ns]
    a_bd = [jnp.where(strict, kq[s][:tile] * decay[s], 0.0) for s in ns]
    qk16 = [(kq[s][tile:] * decay[s]).astype(BF16) for s in ns]
    pw = [-(a_bd[s][0:chunk] + a_bd[s][chunk:2 * chunk] + a_bd[s][2 * chunk:3 * chunk] + a_bd[s][3 * chunk:])
          for s in ns]
    res = list(pw)
    pw = [_dot(pw[s].astype(BF16), block_diag(pw[s])) for s in ns]
    step = 2
    while step < chunk:
        both = [_dot(jnp.concatenate([res[s], pw[s]], axis=0).astype(BF16), block_diag(pw[s])) for s in ns]
        res = [res[s] + pw[s] + both[s][:chunk] for s in ns]
        pw = [both[s][chunk:] for s in ns]
        step *= 2
    egc = [jnp.exp(gc_b[s]) for s in ns]
    rhs = [jnp.concatenate([v[s] * beta[s], k_beta[s] * egc[s]], axis=1) for s in ns]
    sol = [rhs[s] + _dot(block_diag(res[s]), rhs[s].astype(BF16)) for s in ns]
    sol16 = [sol[s].astype(BF16) for s in ns]
    qkuw = [_dot(qk16[s], sol16[s]) for s in ns]
    kd16 = [(k[s] * jnp.exp(gc_last[s] - gc_b[s])).astype(BF16) for s in ns]
    for s in ns:
        hl = hls[s]
        o0_ref[0, r0s[s]:r0s[s] + tile, hl] = qkuw[s][:, :dh].astype(o0_ref.dtype)
        qe16 = (q[s] * egc[s] - qkuw[s][:, dh:]).astype(BF16)
        gl = jnp.exp(gc_last[s])
        for c in range(cpt):
            cr = slice(c * chunk, (c + 1) * chunk)
            ci = streams[s][0] * cpt + c
            nkw = _dot_tn(kd16[s][cr], sol16[s][cr])
            n_ref[0, 0, ci, :, hl] = nkw[:, :dh].astype(n_ref.dtype)
            kwq_ref[0, 0, ci, 0:dh, hl] = nkw[:, dh:].astype(BF16)
            kwq_ref[0, 0, ci, dh:, hl] = qe16[cr]
            gl_ref[0, 0, ci, :, hl] = gl[c * chunk:c * chunk + 8]


def _gdn_scan_kernel(kwq_ref, n_ref, o0_ref, gl_ref, z_ref, nw_ref, o_ref, s_ref, *, hg):
    chunk, dh = GDN_CHUNK, GDN_HEAD_DIM
    groups = kwq_ref.shape[1]
    n_chunks = kwq_ref.shape[2]
    width = hg * dh

    @pl.when(pl.program_id(1) == 0)
    def _():
        s_ref[...] = jnp.zeros_like(s_ref)

    zero = jnp.zeros((dh, dh), BF16)

    def scan(n, carry):
        rows = pl.ds(pl.multiple_of(n * chunk, chunk), chunk)
        state = [s_ref[p] for p in range(groups)]
        s16 = [st.astype(BF16) for st in state]
        s_bd = [jnp.concatenate([jnp.concatenate([x[:, :dh], zero], axis=1),
                                 jnp.concatenate([zero, x[:, dh:]], axis=1)], axis=0) for x in s16]
        both = [_dot(kwq_ref[0, p, n], s_bd[p]) for p in range(groups)]
        for p in range(groups):
            s_ref[p] = (state[p] * jnp.concatenate([gl_ref[0, p, n]] * (dh // 8), axis=0)
                        + n_ref[0, p, n].astype(F32) - both[p][:dh])
        for p in range(groups):
            o = both[p][dh:] + o0_ref[0, rows, p * width:(p + 1) * width].astype(F32)
            for h in range(hg):
                hl = slice(p * width + h * dh, p * width + (h + 1) * dh)
                oh = o[:, h * dh:(h + 1) * dh]
                oh = oh * lax.rsqrt(jnp.mean(oh * oh, axis=-1, keepdims=True) + EPS) * nw_ref[...]
                zg = z_ref[0, rows, hl].astype(F32)
                o_ref[0, rows, hl] = (oh * (zg * jax.nn.sigmoid(zg))).astype(o_ref.dtype)
        return carry

    lax.fori_loop(0, n_chunks, scan, 0)


def _gdn(proj, gb, norm_w, n_batch, seq, n_heads):
    hg = 2
    dh, chunk = GDN_HEAD_DIM, GDN_CHUNK
    width = hg * dh
    groups = n_heads // hg
    mix = n_heads * dh
    rows = GDN_TILE * GDN_TILES_PER_STEP
    cps = rows // chunk
    n_chunks = seq // chunk
    col = lambda base: (lambda b, p, t: (b, t, base * groups + p))
    per_chunk = lambda r: pl.BlockSpec((1, 1, cps, r, width), lambda b, p, t: (b, p, t, 0, 0))
    kwq, n_op, o0, gl = pl.pallas_call(
        functools.partial(_gdn_ops_kernel, n_heads=n_heads, hg=hg),
        grid=(n_batch, groups, seq // rows),
        in_specs=[pl.BlockSpec((1, rows, width), col(0)), pl.BlockSpec((1, rows, width), col(1)),
                  pl.BlockSpec((1, rows, width), col(2)),
                  pl.BlockSpec((1, rows, LANES), lambda b, p, t: (b, t, 0))],
        out_specs=[per_chunk(dh + chunk), per_chunk(dh),
                   pl.BlockSpec((1, rows, width), lambda b, p, t: (b, t, p)), per_chunk(8)],
        out_shape=[jax.ShapeDtypeStruct((n_batch, groups, n_chunks, dh + chunk, width), BF16),
                   jax.ShapeDtypeStruct((n_batch, groups, n_chunks, dh, width), BF16),
                   jax.ShapeDtypeStruct((n_batch, seq, mix), BF16),
                   jax.ShapeDtypeStruct((n_batch, groups, n_chunks, 8, width), F32)],
        scratch_shapes=[pltpu.VMEM((GDN_TILES_PER_STEP, LANES, GDN_TILE), F32)],
        compiler_params=_params("parallel", "parallel", "parallel"),
        name="gdn_ops",
    )(proj, proj, proj, gb)
    all_pairs = lambda r: pl.BlockSpec((1, groups, cps, r, width), lambda b, t: (b, 0, t, 0, 0))
    return pl.pallas_call(
        functools.partial(_gdn_scan_kernel, hg=hg),
        grid=(n_batch, seq // rows),
        in_specs=[all_pairs(dh + chunk), all_pairs(dh),
                  pl.BlockSpec((1, rows, mix), lambda b, t: (b, t, 0)), all_pairs(8),
                  pl.BlockSpec((1, rows, mix), lambda b, t: (b, t, 3)),
                  _const_spec((1, dh))],
        out_specs=pl.BlockSpec((1, rows, mix), lambda b, t: (b, t, 0)),
        out_shape=jax.ShapeDtypeStruct((n_batch, seq, mix), BF16),
        scratch_shapes=[pltpu.VMEM((groups, dh, width), F32)],
        compiler_params=_params("parallel", "arbitrary"),
        name="gdn_scan",
    )(kwq, n_op, o0, gl, proj, norm_w)


def kernel(x, p, mix_norm_e, w_in_e, pool_w, pool_scale, w_out_e, mix_norm_o, w_in_o, conv_qkv_o, a_log_o,
           dt_bias_o, gdn_norm_o, w_out_o, ffn_norm, w_up, ffn_conv, w_down, ple_norm, w_ple_gate, w_ple,
           final_norm):
    n_batch, seq, d = x.shape
    depth = p.shape[0]
    m = n_batch * seq
    pool_width = len(POOL_WINDOWS) * LANES
    row = lambda a: a.reshape(1, -1)
    xf = x.reshape(m, d)
    for i in range(depth):
        j = i // 2
        if i % 2 == 0:
            sb_width = (w_in_e.shape[2] - pool_width) // 3
            scale = jnp.concatenate([jnp.ones((pool_width,), F32),
                                     jnp.full((sb_width,), SB_HEAD_DIM ** -0.5, F32),
                                     jnp.ones((2 * sb_width,), F32)])
            proj = _norm_matmul(xf, row(mix_norm_e[j]), (w_in_e[j] * scale).astype(BF16))
            pc, sc = pool_width // LANES, sb_width // LANES
            attn = _sb_attention(proj.reshape(n_batch, seq, -1), n_batch, seq, pc, pc + sc, pc + 2 * sc, sc)
            mixer = ("even", proj, attn.reshape(m, sb_width), pool_w[j].astype(BF16), row(pool_scale[j]),
                     w_out_e[j].astype(BF16))
        else:
            n_heads = a_log_o.shape[1]
            mix = n_heads * GDN_HEAD_DIM
            wba = jnp.pad(w_in_o[j][:, 4 * mix:], ((0, 0), (0, LANES - 2 * n_heads))).astype(BF16)
            lane_pad = lambda a: jnp.pad(a, (n_heads, LANES - 2 * n_heads)).reshape(1, LANES)
            proj, gb = _odd_in_proj(xf, row(mix_norm_o[j]), w_in_o[j][:, :4 * mix].astype(BF16), conv_qkv_o[j],
                                    wba, lane_pad(a_log_o[j]), lane_pad(dt_bias_o[j]), n_heads, seq)
            o = _gdn(proj.reshape(n_batch, seq, -1), gb.reshape(n_batch, seq, LANES), row(gdn_norm_o[j]),
                     n_batch, seq, n_heads)
            mixer = ("odd", o.reshape(m, mix), w_out_o[j].astype(BF16))
        xf = _layer_tail(xf, mixer, p[i].reshape(m, -1), row(ffn_norm[i]), w_up[i].astype(BF16), ffn_conv[i],
                         w_down[i].astype(BF16), row(ple_norm[i]), w_ple_gate[i].astype(BF16),
                         w_ple[i].astype(BF16), row(final_norm), seq, i == depth - 1)
    return xf.reshape(n_batch, seq, d)
```

```python
import functools

import jax
import jax.numpy as jnp
from jax import lax
from jax.experimental import pallas as pl
from jax.experimental.pallas import tpu as pltpu

F32 = jnp.float32
BF16 = jnp.bfloat16

EPS = 1e-6
LANES = 128
SUBLANES = 8
POOL_WINDOWS = (2, 4, 8, 16)
POOL_HALO = 16
SB_HEAD_DIM = 64
SB_TILE = 128
SB_GROUP = 4
SB_LOG_UNDERFLOW = 88.0
GDN_HEAD_DIM = 128
GDN_CHUNK = 64
GDN_TILE = 256
GDN_TILES_PER_STEP = 4
GDN_CONV = 4
GDN_HALO = 16
FFN_CONV = 3
CONV_HALO = 8
ROW_TILE = 512
FFN_CHUNK = 256
VMEM_LIMIT_BYTES = 56 * 1024 * 1024


def _params(*sem):
    return pltpu.CompilerParams(dimension_semantics=sem, vmem_limit_bytes=VMEM_LIMIT_BYTES)


def _const_spec(shape):
    nd = len(shape)
    return pl.BlockSpec(shape, lambda *_: (0,) * nd, pipeline_mode=pl.Buffered(1))


def _rms(x, gain):
    return x * lax.rsqrt(jnp.mean(x * x, axis=-1, keepdims=True) + EPS) * gain


def _dot(a, b):
    return jnp.dot(a, b, preferred_element_type=F32)


def _dot_nt(a, b):
    return lax.dot_general(a, b, (((1,), (1,)), ((), ())), preferred_element_type=F32)


def _dot_tn(a, b):
    return lax.dot_general(a, b, (((0,), (0,)), ((), ())), preferred_element_type=F32)


def _split_bf16(x):
    hi = x.astype(BF16)
    lo = (x - hi.astype(F32)).astype(BF16)
    return hi, lo


def _norm_matmul_kernel(x_ref, g_ref, w_ref, o_ref, *, n_chunk):
    xn = _rms(x_ref[...], g_ref[...]).astype(BF16)
    for c in range(0, w_ref.shape[1], n_chunk):
        o_ref[:, c:c + n_chunk] = _dot(xn, w_ref[:, c:c + n_chunk]).astype(o_ref.dtype)


def _norm_matmul(x, gain, w, *, tm=ROW_TILE, n_chunk=512):
    m, d = x.shape
    n = w.shape[1]
    return pl.pallas_call(
        functools.partial(_norm_matmul_kernel, n_chunk=n_chunk),
        grid=(m // tm,),
        in_specs=[pl.BlockSpec((tm, d), lambda r: (r, 0)), _const_spec((1, d)), _const_spec((d, n))],
        out_specs=pl.BlockSpec((tm, n), lambda r: (r, 0)),
        out_shape=jax.ShapeDtypeStruct((m, n), BF16),
        compiler_params=_params("parallel"),
        name="norm_matmul",
    )(x, gain, w)


def _odd_in_kernel(x_ref, g_ref, w_ref, cw_ref, wba_ref, alog_ref, dt_ref, o_ref, gb_ref, xs_ref, perm_ref,
                   carry_ref, *, n_chunk, n_heads, seq):
    tm = x_ref.shape[0]
    mix = n_heads * GDN_HEAD_DIM
    nb = tm // SUBLANES
    hist = (GDN_CONV - 1) * SUBLANES
    first = (pl.program_id(0) * tm) % seq == 0
    sub0 = lax.broadcasted_iota(jnp.int32, (SUBLANES, n_chunk), 0) == 0
    def scattered(a):
        return pl.ds((nb * (a % SUBLANES) + a // SUBLANES), SUBLANES, stride=SUBLANES)

    slabs = x_ref.shape[1] // LANES
    for j in range(slabs):
        for a in range(nb):
            xs_ref[j, scattered(a), :] = x_ref[a * SUBLANES:(a + 1) * SUBLANES, j * LANES:(j + 1) * LANES]
    xn = _rms(jnp.concatenate([xs_ref[j] for j in range(slabs)], axis=1), g_ref[...]).astype(BF16)

    def time_order(y):
        width = y.shape[1] // LANES
        for j in range(width):
            perm_ref[j] = y[:, j * LANES:(j + 1) * LANES]
        return jnp.concatenate(
            [jnp.concatenate([perm_ref[j, scattered(a), :] for j in range(width)], axis=1) for a in range(nb)],
            axis=0)

    for c in range(0, w_ref.shape[1], n_chunk):
        cs = slice(c, c + n_chunk)
        y = _dot(xn, w_ref[:, cs])
        if c < 3 * mix:
            prev = carry_ref[:, cs]
            prev = jnp.where(first, jnp.zeros_like(prev), prev)
            tail = y[tm - hist:, :]
            carry_ref[:, cs] = tail
            wrap = jnp.concatenate(
                [jnp.where(sub0, pltpu.roll(prev[b:b + SUBLANES], 1, axis=0),
                           pltpu.roll(tail[b:b + SUBLANES], 1, axis=0)) for b in range(0, hist, SUBLANES)], axis=0)
            w = cw_ref[:, cs]
            z = w[GDN_CONV - 1:GDN_CONV] * y
            for i in range(1, GDN_CONV):
                back = jnp.concatenate([wrap[hist - i * SUBLANES:], y[:tm - i * SUBLANES]], axis=0)
                z = z + w[GDN_CONV - 1 - i:GDN_CONV - i] * back
            y = z * jax.nn.sigmoid(z)
            if c < 2 * mix:
                scale = GDN_HEAD_DIM ** -0.5 if c < mix else 1.0
                heads = []
                for hc in range(0, n_chunk, GDN_HEAD_DIM):
                    yh = y[:, hc:hc + GDN_HEAD_DIM]
                    inv = lax.rsqrt(jnp.sum(yh * yh, axis=-1, keepdims=True) + EPS)
                    heads.append(yh * (inv * scale))
                y = jnp.concatenate(heads, axis=1)
        o_ref[:, cs] = time_order(y).astype(o_ref.dtype)
    ba = time_order(_dot(xn, wba_ref[...]))
    lane = lax.broadcasted_iota(jnp.int32, ba.shape, 1)
    is_g = jnp.logical_and(lane >= n_heads, lane < 2 * n_heads)
    g = jnp.where(is_g, -jnp.exp(alog_ref[...]) * jax.nn.softplus(ba + dt_ref[...]), 0.0)
    beta = jax.nn.sigmoid(ba)
    sub = GDN_TILE
    row = lax.broadcasted_iota(jnp.int32, (sub, sub), 0)
    col = lax.broadcasted_iota(jnp.int32, (sub, sub), 1)
    tri = jnp.where(jnp.logical_and(row // GDN_CHUNK == col // GDN_CHUNK, row >= col), 1.0, 0.0).astype(BF16)
    tri2 = jnp.concatenate([tri, tri], axis=1)
    is_beta = lax.broadcasted_iota(jnp.int32, (sub, LANES), 1) < n_heads
    for r in range(0, ba.shape[0], sub):
        hi, lo = _split_bf16(g[r:r + sub])
        gc = _dot(tri2, jnp.concatenate([hi, lo], axis=0))
        gb_ref[r:r + sub, :] = jnp.where(is_beta, beta[r:r + sub], gc)


def _odd_in_proj(x, gain, w, conv_w, wba, alog_lane, dt_lane, n_heads, seq, *, tm=ROW_TILE, n_chunk=512):
    m, d = x.shape
    n = w.shape[1]
    return pl.pallas_call(
        functools.partial(_odd_in_kernel, n_chunk=n_chunk, n_heads=n_heads, seq=seq),
        grid=(m // tm,),
        in_specs=[pl.BlockSpec((tm, d), lambda r: (r, 0)), _const_spec((1, d)), _const_spec((d, n)),
                  _const_spec(conv_w.shape), _const_spec((d, LANES)), _const_spec((1, LANES)),
                  _const_spec((1, LANES))],
        out_specs=[pl.BlockSpec((tm, n), lambda r: (r, 0)), pl.BlockSpec((tm, LANES), lambda r: (r, 0))],
        out_shape=[jax.ShapeDtypeStruct((m, n), BF16), jax.ShapeDtypeStruct((m, LANES), F32)],
        scratch_shapes=[pltpu.VMEM((d // LANES, tm, LANES), F32), pltpu.VMEM((n_chunk // LANES, tm, LANES), F32),
                        pltpu.VMEM(((GDN_CONV - 1) * SUBLANES, conv_w.shape[1]), F32)],
        compiler_params=_params("arbitrary"),
        name="odd_in_proj",
    )(x, gain, w, conv_w, wba, alog_lane, dt_lane)


def _sb_attn_kernel(q_ref, k_ref, v_ref, o_ref, acc_ref, c_ref):
    t = SB_TILE
    seq = q_ref.shape[1]
    lane = lax.broadcasted_iota(jnp.int32, (t, LANES), 1)
    head0 = lane < SB_HEAD_DIM
    row = lax.broadcasted_iota(jnp.int32, (t, t), 0)
    col = lax.broadcasted_iota(jnp.int32, (t, t), 1)
    r2 = lax.broadcasted_iota(jnp.int32, (2 * t, 2 * t), 0) % t
    c2 = lax.broadcasted_iota(jnp.int32, (2 * t, 2 * t), 1)
    cum = jnp.where((c2 >= t) | (r2 > c2), 1.0, 0.0).astype(BF16)

    below_diag = col < row

    def rows_of(i):
        return pl.ds(i * t, t) if isinstance(i, int) else pl.ds(pl.multiple_of(i * t, t), t)

    def split_heads(x):
        zero = jnp.zeros_like(x)
        return (jnp.where(head0, x, zero), jnp.where(head0, zero, x))

    def log_sigmoids(qm, kb):
        z = _dot_nt(qm, kb)
        soft = jnp.log(1.0 + jnp.exp(-jnp.abs(z)))
        return jnp.minimum(z, 0.0) - soft, jnp.minimum(-z, 0.0) - soft

    def running(ls_neg):
        hi, lo = _split_bf16(ls_neg)
        return _dot(jnp.concatenate([hi, lo], axis=1), cum)

    def sweep(i, slot, j_start, top_start):
        qh = split_heads(q_ref[0, rows_of(i), :])

        def cond(st):
            j, top = st
            return jnp.logical_and(j >= 0, top > -SB_LOG_UNDERFLOW)

        def body(st):
            j, _ = st
            kb = k_ref[0, rows_of(j), :]
            vh = split_heads(v_ref[0, rows_of(j), :])
            valid = (col - row) < (i - j) * t
            acc = acc_ref[slot]
            top = jnp.float32(-jnp.inf)
            for h in range(2):
                ls_pos, ls_neg = log_sigmoids(qh[h], kb)
                run = running(jnp.where(valid, ls_neg, 0.0))
                c_old = c_ref[slot, h]
                a = jnp.where(valid, jnp.exp(ls_pos + run[:, :t] + c_old), 0.0)
                acc = acc + _dot(a.astype(BF16), vh[h])
                c_new = c_old + run[:, t:]
                c_ref[slot, h] = c_new
                top = jnp.maximum(top, jnp.max(c_new))
            acc_ref[slot] = acc
            return j - 1, top

        lax.while_loop(cond, body, (j_start, top_start))
        o_ref[0, rows_of(i), :] = acc_ref[slot].astype(o_ref.dtype)

    def tile_group(first):
        tiles = [first + a for a in range(SB_GROUP)]
        blocks = [[0] if isinstance(i, int) and i == 0 else [0, 1] for i in tiles]
        streams = [(a, b, h) for a in range(SB_GROUP) for b in blocks[a] for h in range(2)]
        qh = [split_heads(q_ref[0, rows_of(i), :]) for i in tiles]
        kb = [[k_ref[0, rows_of(i - b), :] for b in blocks[a]] for a, i in enumerate(tiles)]
        vh = [[split_heads(v_ref[0, rows_of(i - b), :]) for b in blocks[a]] for a, i in enumerate(tiles)]
        ls = {s: log_sigmoids(qh[s[0]][s[2]], kb[s[0]][s[1]]) for s in streams}
        run = {s: running(jnp.where(below_diag, ls[s][1], 0.0) if s[1] == 0 else ls[s][1]) for s in streams}
        tops = []
        for a in range(SB_GROUP):
            acc = jnp.zeros((t, LANES), F32)
            top = jnp.float32(-jnp.inf)
            for h in range(2):
                d = (a, 0, h)
                weights = [jnp.where(below_diag, jnp.exp(ls[d][0] + run[d][:, :t]), 0.0)]
                c_new = run[d][:, t:]
                if len(blocks[a]) == 2:
                    l = (a, 1, h)
                    weights.append(jnp.exp(ls[l][0] + run[l][:, :t] + c_new))
                    c_new = c_new + run[l][:, t:]
                acc = acc + _dot(jnp.concatenate(weights, axis=1).astype(BF16),
                                 jnp.concatenate([vh[a][b][h] for b in blocks[a]], axis=0))
                c_ref[a, h] = c_new
                top = jnp.maximum(top, jnp.max(c_new))
            acc_ref[a] = acc
            tops.append(top)
        for a in range(SB_GROUP):
            sweep(tiles[a], a, jnp.int32(tiles[a] - 2), tops[a])

    tile_group(0)

    def later_group(g, carry):
        tile_group(g * SB_GROUP)
        return carry

    lax.fori_loop(1, seq // (SB_GROUP * t), later_group, 0)


def _sb_attention(proj, n_batch, seq, q_col, k_col, v_col, n_pairs):
    blk = (1, seq, LANES)
    return pl.pallas_call(
        _sb_attn_kernel,
        grid=(n_batch, n_pairs),
        in_specs=[pl.BlockSpec(blk, lambda b, p: (b, 0, q_col + p)),
                  pl.BlockSpec(blk, lambda b, p: (b, 0, k_col + p)),
                  pl.BlockSpec(blk, lambda b, p: (b, 0, v_col + p))],
        out_specs=pl.BlockSpec(blk, lambda b, p: (b, 0, p)),
        out_shape=jax.ShapeDtypeStruct((n_batch, seq, n_pairs * LANES), BF16),
        scratch_shapes=[pltpu.VMEM((SB_GROUP, SB_TILE, LANES), F32),
                        pltpu.VMEM((SB_GROUP, 2, SB_TILE, LANES), F32)],
        compiler_params=_params("parallel", "parallel"),
        name="sb_attention",
    )(proj, proj, proj)


def _pool_mixer_out(u_ref, halo_ref, attn_ref, pw_ref, ps_ref, wo_ref, s_ref, seq):
    tm = u_ref.shape[0]
    t0 = (pl.program_id(0) * tm) % seq
    u = u_ref[...].astype(F32)
    halo = halo_ref[...].astype(F32)
    s_ref[0:POOL_HALO, :] = jnp.where(t0 == 0, jnp.zeros_like(halo), halo)
    s_ref[POOL_HALO:, :] = u
    tpos = t0 + lax.broadcasted_iota(jnp.int32, (tm, 1), 0)
    parts = []
    for g, win in enumerate(POOL_WINDOWS):
        cs = slice(g * LANES, (g + 1) * LANES)
        ug = u[:, cs]
        ws = ug
        for k in range(1, win):
            ws = ws + s_ref[POOL_HALO - k:POOL_HALO - k + tm, cs]
        count = jnp.minimum(tpos + 1, win).astype(F32)
        y = ws / count - ug
        parts.append((_dot(y.astype(BF16), pw_ref[g]) * ps_ref[:, cs]).astype(BF16))
    cat = jnp.concatenate(parts + [attn_ref[...]], axis=1)
    return _dot(cat, wo_ref[...])


def _ffn_ple_tail(x, p_ref, fg_ref, wup_ref, cw_ref, wdn_ref, pg_ref, wg_ref, wp_ref, fn_ref, o_ref,
                  h_ref, ybuf_ref, carry_ref, *, seq, tf, final_norm):
    tm = x.shape[0]
    ffn = wdn_ref.shape[0]
    first = (pl.program_id(0) * tm) % seq == 0
    xn = _rms(x, fg_ref[...]).astype(BF16)
    for c in range(ffn // tf):
        conv = []
        for part in range(2):
            cs = slice(part * ffn + c * tf, part * ffn + (c + 1) * tf)
            y = _dot(xn, wup_ref[:, cs])
            prev = carry_ref[:, cs]
            ybuf_ref[part, 0:CONV_HALO, :] = jnp.where(first, jnp.zeros_like(prev), prev)
            ybuf_ref[part, CONV_HALO:, :] = y
            carry_ref[:, cs] = y[tm - CONV_HALO:, :]
            w = cw_ref[:, cs]
            conv.append(w[2:3] * y
                        + w[1:2] * ybuf_ref[part, CONV_HALO - 1:CONV_HALO - 1 + tm, :]
                        + w[0:1] * ybuf_ref[part, CONV_HALO - 2:CONV_HALO - 2 + tm, :])
        h_ref[:, c * tf:(c + 1) * tf] = (conv[0] * jax.nn.sigmoid(conv[0]) * conv[1]).astype(BF16)
    x2 = x + _dot(h_ref[...], wdn_ref[...])
    gate = jax.nn.sigmoid(_dot(_rms(x2, pg_ref[...]).astype(BF16), wg_ref[...]))
    x3 = x2 + _dot(p_ref[...].astype(BF16), wp_ref[...]) * gate
    if final_norm:
        x3 = _rms(x3, fn_ref[...])
    o_ref[...] = x3


def _even_tail_kernel(x_ref, u_ref, halo_ref, attn_ref, pw_ref, ps_ref, wo_ref, *rest, seq, tf, final_norm):
    *ffn_refs, s_ref, h_ref, ybuf_ref, carry_ref = rest
    x = x_ref[...] + _pool_mixer_out(u_ref, halo_ref, attn_ref, pw_ref, ps_ref, wo_ref, s_ref, seq)
    _ffn_ple_tail(x, *ffn_refs, h_ref, ybuf_ref, carry_ref, seq=seq, tf=tf, final_norm=final_norm)


def _odd_tail_kernel(x_ref, a_ref, wo_ref, *rest, seq, tf, final_norm):
    x = x_ref[...] + _dot(a_ref[...], wo_ref[...])
    _ffn_ple_tail(x, *rest, seq=seq, tf=tf, final_norm=final_norm)


def _layer_tail(x, mixer, p, ffn_gain, w_up, conv_w, w_down, ple_gain, w_gate, w_ple, final_gain, seq, final_norm,
                *, tm=ROW_TILE, tf=FFN_CHUNK):
    m, d = x.shape
    row = lambda r: (r, 0)
    scratch = [pltpu.VMEM((tm, w_down.shape[0]), BF16), pltpu.VMEM((2, tm + CONV_HALO, tf), F32),
               pltpu.VMEM((CONV_HALO, w_up.shape[1]), F32)]
    if mixer[0] == "even":
        _, proj, attn, pool_w, pool_scale, w_out = mixer
        pw = len(POOL_WINDOWS) * LANES
        hb = tm // POOL_HALO
        body = _even_tail_kernel
        mixer_args = (proj, proj, attn, pool_w, pool_scale, w_out)
        mixer_specs = [pl.BlockSpec((tm, pw), row),
                       pl.BlockSpec((POOL_HALO, pw), lambda r: (jnp.maximum(r * hb - 1, 0), 0)),
                       pl.BlockSpec((tm, attn.shape[1]), row),
                       _const_spec(pool_w.shape), _const_spec(pool_scale.shape), _const_spec(w_out.shape)]
        scratch = [pltpu.VMEM((tm + POOL_HALO, pw), F32)] + scratch
    else:
        _, o, w_out = mixer
        body = _odd_tail_kernel
        mixer_args = (o, w_out)
        mixer_specs = [pl.BlockSpec((tm, o.shape[1]), row), _const_spec(w_out.shape)]
    return pl.pallas_call(
        functools.partial(body, seq=seq, tf=tf, final_norm=final_norm),
        grid=(m // tm,),
        in_specs=[pl.BlockSpec((tm, d), row)] + mixer_specs
                 + [pl.BlockSpec((tm, p.shape[1]), row),
                    _const_spec((1, d)), _const_spec(w_up.shape), _const_spec(conv_w.shape),
                    _const_spec(w_down.shape), _const_spec((1, d)), _const_spec(w_gate.shape),
                    _const_spec(w_ple.shape), _const_spec((1, d))],
        out_specs=pl.BlockSpec((tm, d), row),
        out_shape=jax.ShapeDtypeStruct((m, d), F32),
        scratch_shapes=scratch,
        compiler_params=_params("arbitrary"),
        name=mixer[0] + "_tail",
    )(x, *mixer_args, p, ffn_gain, w_up, conv_w, w_down, ple_gain, w_gate, w_ple, final_gain)


def _gdn_ops_kernel(q_ref, k_ref, v_ref, gb_ref, kwq_ref, n_ref, o0_ref, gl_ref, tr_ref, *, n_heads, hg):
    tile, chunk, dh = GDN_TILE, GDN_CHUNK, GDN_HEAD_DIM
    cpt = tile // chunk
    row = lax.broadcasted_iota(jnp.int32, (tile, tile), 0)
    col = lax.broadcasted_iota(jnp.int32, (tile, tile), 1)
    same = (row // chunk) == (col // chunk)
    incl = jnp.logical_and(same, row >= col)
    strict = jnp.logical_and(same, row > col)
    lane = lax.broadcasted_iota(jnp.int32, (tile, LANES), 1)

    def block_diag(wide):
        return jnp.where(same, jnp.concatenate([wide] * cpt, axis=0), 0.0).astype(BF16)

    streams = [(dt, h) for dt in range(GDN_TILES_PER_STEP) for h in range(hg)]
    ns = range(len(streams))
    hls = [slice(h * dh, (h + 1) * dh) for _, h in streams]
    heads = [pl.program_id(1) * hg + h for _, h in streams]

    r0s = [dt * tile for dt, _ in streams]
    gbts = [gb_ref[0, r0:r0 + tile, :] for r0 in r0s]
    for s in ns:
        if streams[s][1] == 0:
            tr_ref[streams[s][0]] = gbts[s].T
    q = [q_ref[0, r0s[s]:r0s[s] + tile, hls[s]].astype(F32) for s in ns]
    k16 = [k_ref[0, r0s[s]:r0s[s] + tile, hls[s]] for s in ns]
    k = [k16[s].astype(F32) for s in ns]
    v = [v_ref[0, r0s[s]:r0s[s] + tile, hls[s]].astype(F32) for s in ns]
    beta = [jnp.sum(jnp.where(lane == heads[s], gbts[s], 0.0), axis=-1, keepdims=True) for s in ns]
    gc = [jnp.sum(jnp.where(lane == n_heads + heads[s], gbts[s], 0.0), axis=-1, keepdims=True) for s in ns]
    gc_b = [jnp.broadcast_to(gc[s], (tile, dh)) for s in ns]
    gc_last = [jnp.broadcast_to(gc_b[s].reshape(cpt, chunk, dh)[:, chunk - 1:chunk, :],
                                (cpt, chunk, dh)).reshape(tile, dh) for s in ns]
    gc_row = [tr_ref[streams[s][0], pl.ds(n_heads + heads[s], 1), :] for s in ns]
    decay = [jnp.where(incl, jnp.exp(jnp.where(incl, gc[s] - gc_row[s], 0.0)), 0.0) for s in ns]
    k_beta = [k[s] * beta[s] for s in ns]
    kq = [_dot_nt(jnp.concatenate([k_beta[s].astype(BF16), q[s].astype(BF16)], axis=0), k16[s]) for s in ns]
    a_bd = [jnp.where(strict, kq[s][:tile] * decay[s], 0.0) for s in ns]
    qk16 = [(kq[s][tile:] * decay[s]).astype(BF16) for s in ns]
    pw = [-(a_bd[s][0:chunk] + a_bd[s][chunk:2 * chunk] + a_bd[s][2 * chunk:3 * chunk] + a_bd[s][3 * chunk:])
          for s in ns]
    res = list(pw)
    pw = [_dot(pw[s].astype(BF16), block_diag(pw[s])) for s in ns]
    step = 2
    while step < chunk:
        both = [_dot(jnp.concatenate([res[s], pw[s]], axis=0).astype(BF16), block_diag(pw[s])) for s in ns]
        res = [res[s] + pw[s] + both[s][:chunk] for s in ns]
        pw = [both[s][chunk:] for s in ns]
        step *= 2
    egc = [jnp.exp(gc_b[s]) for s in ns]
    rhs = [jnp.concatenate([v[s] * beta[s], k_beta[s] * egc[s]], axis=1) for s in ns]
    sol = [rhs[s] + _dot(block_diag(res[s]), rhs[s].astype(BF16)) for s in ns]
    sol16 = [sol[s].astype(BF16) for s in ns]
    qkuw = [_dot(qk16[s], sol16[s]) for s in ns]
    kd16 = [(k[s] * jnp.exp(gc_last[s] - gc_b[s])).astype(BF16) for s in ns]
    for s in ns:
        hl = hls[s]
        o0_ref[0, r0s[s]:r0s[s] + tile, hl] = qkuw[s][:, :dh].astype(o0_ref.dtype)
        qe16 = (q[s] * egc[s] - qkuw[s][:, dh:]).astype(BF16)
        gl = jnp.exp(gc_last[s])
        for c in range(cpt):
            cr = slice(c * chunk, (c + 1) * chunk)
            ci = streams[s][0] * cpt + c
            nkw = _dot_tn(kd16[s][cr], sol16[s][cr])
            n_ref[0, 0, ci, :, hl] = nkw[:, :dh].astype(n_ref.dtype)
            kwq_ref[0, 0, ci, 0:dh, hl] = nkw[:, dh:].astype(BF16)
            kwq_ref[0, 0, ci, dh:, hl] = qe16[cr]
            gl_ref[0, 0, ci, :, hl] = gl[c * chunk:c * chunk + 8]


def _gdn_scan_kernel(kwq_ref, n_ref, o0_ref, gl_ref, z_ref, nw_ref, o_ref, s_ref, *, hg):
    chunk, dh = GDN_CHUNK, GDN_HEAD_DIM
    groups = kwq_ref.shape[1]
    n_chunks = kwq_ref.shape[2]
    width = hg * dh

    @pl.when(pl.program_id(1) == 0)
    def _():
        s_ref[...] = jnp.zeros_like(s_ref)

    zero = jnp.zeros((dh, dh), BF16)

    def scan(n, carry):
        rows = pl.ds(pl.multiple_of(n * chunk, chunk), chunk)
        state = [s_ref[p] for p in range(groups)]
        s16 = [st.astype(BF16) for st in state]
        s_bd = [jnp.concatenate([jnp.concatenate([x[:, :dh], zero], axis=1),
                                 jnp.concatenate([zero, x[:, dh:]], axis=1)], axis=0) for x in s16]
        both = [_dot(kwq_ref[0, p, n], s_bd[p]) for p in range(groups)]
        for p in range(groups):
            s_ref[p] = (state[p] * jnp.concatenate([gl_ref[0, p, n]] * (dh // 8), axis=0)
                        + n_ref[0, p, n].astype(F32) - both[p][:dh])
        for p in range(groups):
            o = both[p][dh:] + o0_ref[0, rows, p * width:(p + 1) * width].astype(F32)
            for h in range(hg):
                hl = slice(p * width + h * dh, p * width + (h + 1) * dh)
                oh = o[:, h * dh:(h + 1) * dh]
                oh = oh * lax.rsqrt(jnp.mean(oh * oh, axis=-1, keepdims=True) + EPS) * nw_ref[...]
                zg = z_ref[0, rows, hl].astype(F32)
                o_ref[0, rows, hl] = (oh * (zg * jax.nn.sigmoid(zg))).astype(o_ref.dtype)
        return carry

    lax.fori_loop(0, n_chunks, scan, 0)


def _gdn(proj, gb, norm_w, n_batch, seq, n_heads):
    hg = 2
    dh, chunk = GDN_HEAD_DIM, GDN_CHUNK
    width = hg * dh
    groups = n_heads // hg
    mix = n_heads * dh
    rows = GDN_TILE * GDN_TILES_PER_STEP
    cps = rows // chunk
    n_chunks = seq // chunk
    col = lambda base: (lambda b, p, t: (b, t, base * groups + p))
    per_chunk = lambda r: pl.BlockSpec((1, 1, cps, r, width), lambda b, p, t: (b, p, t, 0, 0))
    kwq, n_op, o0, gl = pl.pallas_call(
        functools.partial(_gdn_ops_kernel, n_heads=n_heads, hg=hg),
        grid=(n_batch, groups, seq // rows),
        in_specs=[pl.BlockSpec((1, rows, width), col(0)), pl.BlockSpec((1, rows, width), col(1)),
                  pl.BlockSpec((1, rows, width), col(2)),
                  pl.BlockSpec((1, rows, LANES), lambda b, p, t: (b, t, 0))],
        out_specs=[per_chunk(dh + chunk), per_chunk(dh),
                   pl.BlockSpec((1, rows, width), lambda b, p, t: (b, t, p)), per_chunk(8)],
        out_shape=[jax.ShapeDtypeStruct((n_batch, groups, n_chunks, dh + chunk, width), BF16),
                   jax.ShapeDtypeStruct((n_batch, groups, n_chunks, dh, width), BF16),
                   jax.ShapeDtypeStruct((n_batch, seq, mix), BF16),
                   jax.ShapeDtypeStruct((n_batch, groups, n_chunks, 8, width), F32)],
        scratch_shapes=[pltpu.VMEM((GDN_TILES_PER_STEP, LANES, GDN_TILE), F32)],
        compiler_params=_params("parallel", "parallel", "parallel"),
        name="gdn_ops",
    )(proj, proj, proj, gb)
    all_pairs = lambda r: pl.BlockSpec((1, groups, cps, r, width), lambda b, t: (b, 0, t, 0, 0))
    return pl.pallas_call(
        functools.partial(_gdn_scan_kernel, hg=hg),
        grid=(n_batch, seq // rows),
        in_specs=[all_pairs(dh + chunk), all_pairs(dh),
                  pl.BlockSpec((1, rows, mix), lambda b, t: (b, t, 0)), all_pairs(8),
                  pl.BlockSpec((1, rows, mix), lambda b, t: (b, t, 3)),
                  _const_spec((1, dh))],
        out_specs=pl.BlockSpec((1, rows, mix), lambda b, t: (b, t, 0)),
        out_shape=jax.ShapeDtypeStruct((n_batch, seq, mix), BF16),
        scratch_shapes=[pltpu.VMEM((groups, dh, width), F32)],
        compiler_params=_params("parallel", "arbitrary"),
        name="gdn_scan",
    )(kwq, n_op, o0, gl, proj, norm_w)


def kernel(x, p, mix_norm_e, w_in_e, pool_w, pool_scale, w_out_e, mix_norm_o, w_in_o, conv_qkv_o, a_log_o,
           dt_bias_o, gdn_norm_o, w_out_o, ffn_norm, w_up, ffn_conv, w_down, ple_norm, w_ple_gate, w_ple,
           final_norm):
    n_batch, seq, d = x.shape
    depth = p.shape[0]
    m = n_batch * seq
    pool_width = len(POOL_WINDOWS) * LANES
    row = lambda a: a.reshape(1, -1)
    xf = x.reshape(m, d)
    for i in range(depth):
        j = i // 2
        if i % 2 == 0:
            sb_width = (w_in_e.shape[2] - pool_width) // 3
            scale = jnp.concatenate([jnp.ones((pool_width,), F32),
                                     jnp.full((sb_width,), SB_HEAD_DIM ** -0.5, F32),
                                     jnp.ones((2 * sb_width,), F32)])
            proj = _norm_matmul(xf, row(mix_norm_e[j]), (w_in_e[j] * scale).astype(BF16))
            pc, sc = pool_width // LANES, sb_width // LANES
            attn = _sb_attention(proj.reshape(n_batch, seq, -1), n_batch, seq, pc, pc + sc, pc + 2 * sc, sc)
            mixer = ("even", proj, attn.reshape(m, sb_width), pool_w[j].astype(BF16), row(pool_scale[j]),
                     w_out_e[j].astype(BF16))
        else:
            n_heads = a_log_o.shape[1]
            mix = n_heads * GDN_HEAD_DIM
            wba = jnp.pad(w_in_o[j][:, 4 * mix:], ((0, 0), (0, LANES - 2 * n_heads))).astype(BF16)
            lane_pad = lambda a: jnp.pad(a, (n_heads, LANES - 2 * n_heads)).reshape(1, LANES)
            proj, gb = _odd_in_proj(xf, row(mix_norm_o[j]), w_in_o[j][:, :4 * mix].astype(BF16), conv_qkv_o[j],
                                    wba, lane_pad(a_log_o[j]), lane_pad(dt_bias_o[j]), n_heads, seq)
            o = _gdn(proj.reshape(n_batch, seq, -1), gb.reshape(n_batch, seq, LANES), row(gdn_norm_o[j]),
                     n_batch, seq, n_heads)
            mixer = ("odd", o.reshape(m, mix), w_out_o[j].astype(BF16))
        xf = _layer_tail(xf, mixer, p[i].reshape(m, -1), row(ffn_norm[i]), w_up[i].astype(BF16), ffn_conv[i],
                         w_down[i].astype(BF16), row(ple_norm[i]), w_ple_gate[i].astype(BF16),
                         w_ple[i].astype(BF16), row(final_norm), seq, i == depth - 1)
    return xf.reshape(n_batch, seq, d)
```

```python
import functools

import jax
import jax.numpy as jnp
from jax import lax
from jax.experimental import pallas as pl
from jax.experimental.pallas import tpu as pltpu

F32 = jnp.float32
BF16 = jnp.bfloat16

EPS = 1e-6
LANES = 128
SUBLANES = 8
POOL_WINDOWS = (2, 4, 8, 16)
POOL_HALO = 16
SB_HEAD_DIM = 64
SB_TILE = 128
SB_GROUP = 8
SB_LOG_UNDERFLOW = 88.0
GDN_HEAD_DIM = 128
GDN_CHUNK = 64
GDN_TILE = 256
GDN_TILES_PER_STEP = 4
GDN_SCAN_BATCH = 2
GDN_SCAN_ROWS = 512
GDN_CONV = 4
GDN_HALO = 16
FFN_CONV = 3
CONV_HALO = 8
ROW_TILE = 512
FFN_CHUNK = 256
VMEM_LIMIT_BYTES = 56 * 1024 * 1024


def _params(*sem):
    return pltpu.CompilerParams(dimension_semantics=sem, vmem_limit_bytes=VMEM_LIMIT_BYTES)


def _const_spec(shape):
    nd = len(shape)
    return pl.BlockSpec(shape, lambda *_: (0,) * nd, pipeline_mode=pl.Buffered(1))


def _rms(x, gain):
    return x * lax.rsqrt(jnp.mean(x * x, axis=-1, keepdims=True) + EPS) * gain


def _dot(a, b):
    return jnp.dot(a, b, preferred_element_type=F32)


def _dot_nt(a, b):
    return lax.dot_general(a, b, (((1,), (1,)), ((), ())), preferred_element_type=F32)


def _dot_tn(a, b):
    return lax.dot_general(a, b, (((0,), (0,)), ((), ())), preferred_element_type=F32)


def _split_bf16(x):
    hi = x.astype(BF16)
    lo = (x - hi.astype(F32)).astype(BF16)
    return hi, lo


def _norm_matmul_kernel(x_ref, g_ref, w_ref, o_ref, *, n_chunk):
    xn = _rms(x_ref[...], g_ref[...]).astype(BF16)
    for c in range(0, w_ref.shape[1], n_chunk):
        o_ref[:, c:c + n_chunk] = _dot(xn, w_ref[:, c:c + n_chunk]).astype(o_ref.dtype)


def _norm_matmul(x, gain, w, *, tm=ROW_TILE, n_chunk=512):
    m, d = x.shape
    n = w.shape[1]
    return pl.pallas_call(
        functools.partial(_norm_matmul_kernel, n_chunk=n_chunk),
        grid=(m // tm,),
        in_specs=[pl.BlockSpec((tm, d), lambda r: (r, 0)), _const_spec((1, d)), _const_spec((d, n))],
        out_specs=pl.BlockSpec((tm, n), lambda r: (r, 0)),
        out_shape=jax.ShapeDtypeStruct((m, n), BF16),
        compiler_params=_params("parallel"),
        name="norm_matmul",
    )(x, gain, w)


def _odd_in_kernel(x_ref, g_ref, w_ref, cw_ref, wba_ref, alog_ref, dt_ref, o_ref, gb_ref, xs_ref, perm_ref,
                   carry_ref, *, n_chunk, n_heads, seq):
    tm = x_ref.shape[0]
    mix = n_heads * GDN_HEAD_DIM
    nb = tm // SUBLANES
    hist = (GDN_CONV - 1) * SUBLANES
    first = (pl.program_id(0) * tm) % seq == 0
    sub0 = lax.broadcasted_iota(jnp.int32, (SUBLANES, n_chunk), 0) == 0
    def scattered(a):
        return pl.ds((nb * (a % SUBLANES) + a // SUBLANES), SUBLANES, stride=SUBLANES)

    slabs = x_ref.shape[1] // LANES
    for j in range(slabs):
        for a in range(nb):
            xs_ref[j, scattered(a), :] = x_ref[a * SUBLANES:(a + 1) * SUBLANES, j * LANES:(j + 1) * LANES]
    xn = _rms(jnp.concatenate([xs_ref[j] for j in range(slabs)], axis=1), g_ref[...]).astype(BF16)

    def time_order(y):
        width = y.shape[1] // LANES
        for j in range(width):
            perm_ref[j] = y[:, j * LANES:(j + 1) * LANES]
        return jnp.concatenate(
            [jnp.concatenate([perm_ref[j, scattered(a), :] for j in range(width)], axis=1) for a in range(nb)],
            axis=0)

    for c in range(0, w_ref.shape[1], n_chunk):
        cs = slice(c, c + n_chunk)
        y = _dot(xn, w_ref[:, cs])
        if c < 3 * mix:
            prev = carry_ref[:, cs]
            prev = jnp.where(first, jnp.zeros_like(prev), prev)
            tail = y[tm - hist:, :]
            carry_ref[:, cs] = tail
            wrap = jnp.concatenate(
                [jnp.where(sub0, pltpu.roll(prev[b:b + SUBLANES], 1, axis=0),
                           pltpu.roll(tail[b:b + SUBLANES], 1, axis=0)) for b in range(0, hist, SUBLANES)], axis=0)
            w = cw_ref[:, cs]
            z = w[GDN_CONV - 1:GDN_CONV] * y
            for i in range(1, GDN_CONV):
                back = jnp.concatenate([wrap[hist - i * SUBLANES:], y[:tm - i * SUBLANES]], axis=0)
                z = z + w[GDN_CONV - 1 - i:GDN_CONV - i] * back
            y = z * jax.nn.sigmoid(z)
            if c < 2 * mix:
                scale = GDN_HEAD_DIM ** -0.5 if c < mix else 1.0
                heads = []
                for hc in range(0, n_chunk, GDN_HEAD_DIM):
                    yh = y[:, hc:hc + GDN_HEAD_DIM]
                    inv = lax.rsqrt(jnp.sum(yh * yh, axis=-1, keepdims=True) + EPS)
                    heads.append(yh * (inv * scale))
                y = jnp.concatenate(heads, axis=1)
        o_ref[:, cs] = time_order(y).astype(o_ref.dtype)
    ba = time_order(_dot(xn, wba_ref[...]))
    lane = lax.broadcasted_iota(jnp.int32, ba.shape, 1)
    is_g = jnp.logical_and(lane >= n_heads, lane < 2 * n_heads)
    g = jnp.where(is_g, -jnp.exp(alog_ref[...]) * jax.nn.softplus(ba + dt_ref[...]), 0.0)
    beta = jax.nn.sigmoid(ba)
    sub = GDN_TILE
    row = lax.broadcasted_iota(jnp.int32, (sub, sub), 0)
    col = lax.broadcasted_iota(jnp.int32, (sub, sub), 1)
    tri = jnp.where(jnp.logical_and(row // GDN_CHUNK == col // GDN_CHUNK, row >= col), 1.0, 0.0).astype(BF16)
    tri2 = jnp.concatenate([tri, tri], axis=1)
    is_beta = lax.broadcasted_iota(jnp.int32, (sub, LANES), 1) < n_heads
    for r in range(0, ba.shape[0], sub):
        hi, lo = _split_bf16(g[r:r + sub])
        gc = _dot(tri2, jnp.concatenate([hi, lo], axis=0))
        gb_ref[r:r + sub, :] = jnp.where(is_beta, beta[r:r + sub], gc)


def _odd_in_proj(x, gain, w, conv_w, wba, alog_lane, dt_lane, n_heads, seq, *, tm=ROW_TILE, n_chunk=512):
    m, d = x.shape
    n = w.shape[1]
    return pl.pallas_call(
        functools.partial(_odd_in_kernel, n_chunk=n_chunk, n_heads=n_heads, seq=seq),
        grid=(m // tm,),
        in_specs=[pl.BlockSpec((tm, d), lambda r: (r, 0)), _const_spec((1, d)), _const_spec((d, n)),
                  _const_spec(conv_w.shape), _const_spec((d, LANES)), _const_spec((1, LANES)),
                  _const_spec((1, LANES))],
        out_specs=[pl.BlockSpec((tm, n), lambda r: (r, 0)), pl.BlockSpec((tm, LANES), lambda r: (r, 0))],
        out_shape=[jax.ShapeDtypeStruct((m, n), BF16), jax.ShapeDtypeStruct((m, LANES), F32)],
        scratch_shapes=[pltpu.VMEM((d // LANES, tm, LANES), F32), pltpu.VMEM((n_chunk // LANES, tm, LANES), F32),
                        pltpu.VMEM(((GDN_CONV - 1) * SUBLANES, conv_w.shape[1]), F32)],
        compiler_params=_params("arbitrary"),
        name="odd_in_proj",
    )(x, gain, w, conv_w, wba, alog_lane, dt_lane)


def _sb_attn_kernel(q_ref, k_ref, v_ref, o_ref, acc_ref, c_ref):
    t = SB_TILE
    seq = q_ref.shape[1]
    lane = lax.broadcasted_iota(jnp.int32, (t, LANES), 1)
    head0 = lane < SB_HEAD_DIM
    row = lax.broadcasted_iota(jnp.int32, (t, t), 0)
    col = lax.broadcasted_iota(jnp.int32, (t, t), 1)
    r2 = lax.broadcasted_iota(jnp.int32, (2 * t, 2 * t), 0) % t
    c2 = lax.broadcasted_iota(jnp.int32, (2 * t, 2 * t), 1)
    cum = jnp.where((c2 >= t) | (r2 > c2), 1.0, 0.0).astype(BF16)

    below_diag = col < row

    def rows_of(i):
        return pl.ds(i * t, t) if isinstance(i, int) else pl.ds(pl.multiple_of(i * t, t), t)

    def split_heads(x):
        zero = jnp.zeros_like(x)
        return (jnp.where(head0, x, zero), jnp.where(head0, zero, x))

    def log_sigmoids(qm, kb):
        z = _dot_nt(qm, kb)
        ls_pos = jnp.minimum(z, 0.0) - jnp.log(1.0 + jnp.exp(-jnp.abs(z)))
        return ls_pos, ls_pos - z

    def running(ls_neg):
        hi, lo = _split_bf16(ls_neg)
        return _dot(jnp.concatenate([hi, lo], axis=1), cum)

    def sweep(i, slot, j_start, top_start):
        qh = split_heads(q_ref[0, rows_of(i), :])

        def cond(st):
            j, top = st
            return jnp.logical_and(j >= 0, top > -SB_LOG_UNDERFLOW)

        def body(st):
            j, _ = st
            kb = k_ref[0, rows_of(j), :]
            vh = split_heads(v_ref[0, rows_of(j), :])
            valid = (col - row) < (i - j) * t
            acc = acc_ref[slot]
            top = jnp.float32(-jnp.inf)
            for h in range(2):
                ls_pos, ls_neg = log_sigmoids(qh[h], kb)
                run = running(jnp.where(valid, ls_neg, 0.0))
                c_old = c_ref[slot, h]
                a = jnp.where(valid, jnp.exp(ls_pos + run[:, :t] + c_old), 0.0)
                acc = acc + _dot(a.astype(BF16), vh[h])
                c_new = c_old + run[:, t:]
                c_ref[slot, h] = c_new
                top = jnp.maximum(top, jnp.max(c_new))
            acc_ref[slot] = acc
            return j - 1, top

        lax.while_loop(cond, body, (j_start, top_start))
        o_ref[0, rows_of(i), :] = acc_ref[slot].astype(o_ref.dtype)

    def tile_group(first):
        tiles = [first + a for a in range(SB_GROUP)]
        blocks = [[0] if isinstance(i, int) and i == 0 else [0, 1] for i in tiles]
        streams = [(a, b, h) for a in range(SB_GROUP) for b in blocks[a] for h in range(2)]
        qh = [split_heads(q_ref[0, rows_of(i), :]) for i in tiles]
        kb = [[k_ref[0, rows_of(i - b), :] for b in blocks[a]] for a, i in enumerate(tiles)]
        vh = [[split_heads(v_ref[0, rows_of(i - b), :]) for b in blocks[a]] for a, i in enumerate(tiles)]
        ls = {s: log_sigmoids(qh[s[0]][s[2]], kb[s[0]][s[1]]) for s in streams}
        run = {s: running(jnp.where(below_diag, ls[s][1], 0.0) if s[1] == 0 else ls[s][1]) for s in streams}
        tops = []
        for a in range(SB_GROUP):
            acc = jnp.zeros((t, LANES), F32)
            top = jnp.float32(-jnp.inf)
            for h in range(2):
                d = (a, 0, h)
                weights = [jnp.where(below_diag, jnp.exp(ls[d][0] + run[d][:, :t]), 0.0)]
                c_new = run[d][:, t:]
                if len(blocks[a]) == 2:
                    l = (a, 1, h)
                    weights.append(jnp.exp(ls[l][0] + run[l][:, :t] + c_new))
                    c_new = c_new + run[l][:, t:]
                acc = acc + _dot(jnp.concatenate(weights, axis=1).astype(BF16),
                                 jnp.concatenate([vh[a][b][h] for b in blocks[a]], axis=0))
                c_ref[a, h] = c_new
                top = jnp.maximum(top, jnp.max(c_new))
            acc_ref[a] = acc
            tops.append(top)
        for a in range(SB_GROUP):
            sweep(tiles[a], a, jnp.int32(tiles[a] - 2), tops[a])

    tile_group(0)

    def later_group(g, carry):
        tile_group(g * SB_GROUP)
        return carry

    lax.fori_loop(1, seq // (SB_GROUP * t), later_group, 0)


def _sb_attention(proj, n_batch, seq, q_col, k_col, v_col, n_pairs):
    blk = (1, seq, LANES)
    return pl.pallas_call(
        _sb_attn_kernel,
        grid=(n_batch, n_pairs),
        in_specs=[pl.BlockSpec(blk, lambda b, p: (b, 0, q_col + p)),
                  pl.BlockSpec(blk, lambda b, p: (b, 0, k_col + p)),
                  pl.BlockSpec(blk, lambda b, p: (b, 0, v_col + p))],
        out_specs=pl.BlockSpec(blk, lambda b, p: (b, 0, p)),
        out_shape=jax.ShapeDtypeStruct((n_batch, seq, n_pairs * LANES), BF16),
        scratch_shapes=[pltpu.VMEM((SB_GROUP, SB_TILE, LANES), F32),
                        pltpu.VMEM((SB_GROUP, 2, SB_TILE, LANES), F32)],
        compiler_params=_params("parallel", "parallel"),
        name="sb_attention",
    )(proj, proj, proj)


def _pool_mixer_out(u_ref, halo_ref, attn_ref, pw_ref, ps_ref, wo_ref, s_ref, seq):
    tm = u_ref.shape[0]
    t0 = (pl.program_id(0) * tm) % seq
    u = u_ref[...].astype(F32)
    halo = halo_ref[...].astype(F32)
    s_ref[0:POOL_HALO, :] = jnp.where(t0 == 0, jnp.zeros_like(halo), halo)
    s_ref[POOL_HALO:, :] = u
    tpos = t0 + lax.broadcasted_iota(jnp.int32, (tm, 1), 0)
    parts = []
    for g, win in enumerate(POOL_WINDOWS):
        cs = slice(g * LANES, (g + 1) * LANES)
        ug = u[:, cs]
        ws = ug
        for k in range(1, win):
            ws = ws + s_ref[POOL_HALO - k:POOL_HALO - k + tm, cs]
        count = jnp.minimum(tpos + 1, win).astype(F32)
        y = ws / count - ug
        parts.append((_dot(y.astype(BF16), pw_ref[g]) * ps_ref[:, cs]).astype(BF16))
    cat = jnp.concatenate(parts + [attn_ref[...]], axis=1)
    return _dot(cat, wo_ref[...])


def _ffn_ple_tail(x, p_ref, fg_ref, wup_ref, cw_ref, wdn_ref, pg_ref, wg_ref, wp_ref, fn_ref, o_ref,
                  h_ref, ybuf_ref, carry_ref, *, seq, tf, final_norm):
    tm = x.shape[0]
    ffn = wdn_ref.shape[0]
    first = (pl.program_id(0) * tm) % seq == 0
    xn = _rms(x, fg_ref[...]).astype(BF16)
    for c in range(ffn // tf):
        conv = []
        for part in range(2):
            cs = slice(part * ffn + c * tf, part * ffn + (c + 1) * tf)
            y = _dot(xn, wup_ref[:, cs])
            prev = carry_ref[:, cs]
            ybuf_ref[part, 0:CONV_HALO, :] = jnp.where(first, jnp.zeros_like(prev), prev)
            ybuf_ref[part, CONV_HALO:, :] = y
            carry_ref[:, cs] = y[tm - CONV_HALO:, :]
            w = cw_ref[:, cs]
            conv.append(w[2:3] * y
                        + w[1:2] * ybuf_ref[part, CONV_HALO - 1:CONV_HALO - 1 + tm, :]
                        + w[0:1] * ybuf_ref[part, CONV_HALO - 2:CONV_HALO - 2 + tm, :])
        h_ref[:, c * tf:(c + 1) * tf] = (conv[0] * jax.nn.sigmoid(conv[0]) * conv[1]).astype(BF16)
    x2 = x + _dot(h_ref[...], wdn_ref[...])
    gate = jax.nn.sigmoid(_dot(_rms(x2, pg_ref[...]).astype(BF16), wg_ref[...]))
    x3 = x2 + _dot(p_ref[...].astype(BF16), wp_ref[...]) * gate
    if final_norm:
        x3 = _rms(x3, fn_ref[...])
    o_ref[...] = x3


def _even_tail_kernel(x_ref, u_ref, halo_ref, attn_ref, pw_ref, ps_ref, wo_ref, *rest, seq, tf, final_norm):
    *ffn_refs, s_ref, h_ref, ybuf_ref, carry_ref = rest
    x = x_ref[...] + _pool_mixer_out(u_ref, halo_ref, attn_ref, pw_ref, ps_ref, wo_ref, s_ref, seq)
    _ffn_ple_tail(x, *ffn_refs, h_ref, ybuf_ref, carry_ref, seq=seq, tf=tf, final_norm=final_norm)


def _odd_tail_kernel(x_ref, a_ref, wo_ref, *rest, seq, tf, final_norm):
    x = x_ref[...] + _dot(a_ref[...], wo_ref[...])
    _ffn_ple_tail(x, *rest, seq=seq, tf=tf, final_norm=final_norm)


def _layer_tail(x, mixer, p, ffn_gain, w_up, conv_w, w_down, ple_gain, w_gate, w_ple, final_gain, seq, final_norm,
                *, tm=ROW_TILE, tf=FFN_CHUNK):
    m, d = x.shape
    row = lambda r: (r, 0)
    layer, p = p
    p_row = lambda r: (layer * (m // tm) + r, 0)
    scratch = [pltpu.VMEM((tm, w_down.shape[0]), BF16), pltpu.VMEM((2, tm + CONV_HALO, tf), F32),
               pltpu.VMEM((CONV_HALO, w_up.shape[1]), F32)]
    if mixer[0] == "even":
        _, proj, attn, pool_w, pool_scale, w_out = mixer
        pw = len(POOL_WINDOWS) * LANES
        hb = tm // POOL_HALO
        body = _even_tail_kernel
        mixer_args = (proj, proj, attn, pool_w, pool_scale, w_out)
        mixer_specs = [pl.BlockSpec((tm, pw), row),
                       pl.BlockSpec((POOL_HALO, pw), lambda r: (jnp.maximum(r * hb - 1, 0), 0)),
                       pl.BlockSpec((tm, attn.shape[1]), row),
                       _const_spec(pool_w.shape), _const_spec(pool_scale.shape), _const_spec(w_out.shape)]
        scratch = [pltpu.VMEM((tm + POOL_HALO, pw), F32)] + scratch
    else:
        _, o, w_out = mixer
        body = _odd_tail_kernel
        mixer_args = (o, w_out)
        mixer_specs = [pl.BlockSpec((tm, o.shape[1]), row), _const_spec(w_out.shape)]
    return pl.pallas_call(
        functools.partial(body, seq=seq, tf=tf, final_norm=final_norm),
        grid=(m // tm,),
        in_specs=[pl.BlockSpec((tm, d), row)] + mixer_specs
                 + [pl.BlockSpec((tm, p.shape[1]), p_row),
                    _const_spec((1, d)), _const_spec(w_up.shape), _const_spec(conv_w.shape),
                    _const_spec(w_down.shape), _const_spec((1, d)), _const_spec(w_gate.shape),
                    _const_spec(w_ple.shape), _const_spec((1, d))],
        out_specs=pl.BlockSpec((tm, d), row),
        out_shape=jax.ShapeDtypeStruct((m, d), F32),
        scratch_shapes=scratch,
        compiler_params=_params("arbitrary"),
        name=mixer[0] + "_tail",
    )(x, *mixer_args, p, ffn_gain, w_up, conv_w, w_down, ple_gain, w_gate, w_ple, final_gain)


def _gdn_ops_kernel(q_ref, k_ref, v_ref, gb_ref, kwq_ref, n_ref, o0_ref, gl_ref, tr_ref, *, n_heads, hg):
    tile, chunk, dh = GDN_TILE, GDN_CHUNK, GDN_HEAD_DIM
    cpt = tile // chunk
    row = lax.broadcasted_iota(jnp.int32, (tile, tile), 0)
    col = lax.broadcasted_iota(jnp.int32, (tile, tile), 1)
    same = (row // chunk) == (col // chunk)
    incl = jnp.logical_and(same, row >= col)
    strict = jnp.logical_and(same, row > col)
    lane = lax.broadcasted_iota(jnp.int32, (tile, LANES), 1)

    def block_diag(wide):
        return jnp.where(same, jnp.concatenate([wide] * cpt, axis=0), 0.0).astype(BF16)

    streams = [(dt, h) for dt in range(GDN_TILES_PER_STEP) for h in range(hg)]
    ns = range(len(streams))
    hls = [slice(h * dh, (h + 1) * dh) for _, h in streams]
    heads = [pl.program_id(1) * hg + h for _, h in streams]

    r0s = [dt * tile for dt, _ in streams]
    gbts = [gb_ref[0, r0:r0 + tile, :] for r0 in r0s]
    for s in ns:
        if streams[s][1] == 0:
            tr_ref[streams[s][0]] = gbts[s].T
    q = [q_ref[0, r0s[s]:r0s[s] + tile, hls[s]].astype(F32) for s in ns]
    k16 = [k_ref[0, r0s[s]:r0s[s] + tile, hls[s]] for s in ns]
    k = [k16[s].astype(F32) for s in ns]
    v = [v_ref[0, r0s[s]:r0s[s] + tile, hls[s]].astype(F32) for s in ns]
    beta = [jnp.sum(jnp.where(lane == heads[s], gbts[s], 0.0), axis=-1, keepdims=True) for s in ns]
    gc = [jnp.sum(jnp.where(lane == n_heads + heads[s], gbts[s], 0.0), axis=-1, keepdims=True) for s in ns]
    gc_b = [jnp.broadcast_to(gc[s], (tile, dh)) for s in ns]
    gc_last = [jnp.broadcast_to(gc_b[s].reshape(cpt, chunk, dh)[:, chunk - 1:chunk, :],
                                (cpt, chunk, dh)).reshape(tile, dh) for s in ns]
    gc_row = [tr_ref[streams[s][0], pl.ds(n_heads + heads[s], 1), :] for s in ns]
    decay = [jnp.where(incl, jnp.exp(jnp.where(incl, gc[s] - gc_row[s], 0.0)), 0.0) for s in ns]
    k_beta = [k[s] * beta[s] for s in ns]
    kq = [_dot_nt(jnp.concatenate([k_beta[s].astype(BF16), q[s].astype(BF16)], axis=0), k16[s]) for s in ns]
    a_bd = [jnp.where(strict, kq[s][:tile] * decay[s], 0.0) for s in ns]
    qk16 = [(kq[s][tile:] * decay[s]).astype(BF16) for s in ns]
    pw = [-(a_bd[s][0:chunk] + a_bd[s][chunk:2 * chunk] + a_bd[s][2 * chunk:3 * chunk] + a_bd[s][3 * chunk:])
          for s in ns]
    res = list(pw)
    pw = [_dot(pw[s].astype(BF16), block_diag(pw[s])) for s in ns]
    step = 2
    while step < chunk:
        both = [_dot(jnp.concatenate([res[s], pw[s]], axis=0).astype(BF16), block_diag(pw[s])) for s in ns]
        res = [res[s] + pw[s] + both[s][:chunk] for s in ns]
        pw = [both[s][chunk:] for s in ns]
        step *= 2
    egc = [jnp.exp(gc_b[s]) for s in ns]
    rhs = [jnp.concatenate([v[s] * beta[s], k_beta[s] * egc[s]], axis=1) for s in ns]
    sol = [rhs[s] + _dot(block_diag(res[s]), rhs[s].astype(BF16)) for s in ns]
    sol16 = [sol[s].astype(BF16) for s in ns]
    qkuw = [_dot(qk16[s], sol16[s]) for s in ns]
    kd16 = [(k[s] * jnp.exp(gc_last[s] - gc_b[s])).astype(BF16) for s in ns]
    for s in ns:
        hl = hls[s]
        o0_ref[0, r0s[s]:r0s[s] + tile, hl] = qkuw[s][:, :dh].astype(o0_ref.dtype)
        qe16 = (q[s] * egc[s] - qkuw[s][:, dh:]).astype(BF16)
        gl = jnp.exp(gc_last[s])
        for c in range(cpt):
            cr = slice(c * chunk, (c + 1) * chunk)
            ci = streams[s][0] * cpt + c
            nkw = _dot_tn(kd16[s][cr], sol16[s][cr])
            n_ref[0, 0, ci, :, hl] = nkw[:, :dh].astype(n_ref.dtype)
            kwq_ref[0, 0, ci, 0:dh, hl] = nkw[:, dh:].astype(BF16)
            kwq_ref[0, 0, ci, dh:, hl] = qe16[cr]
            gl_ref[0, 0, ci, :, hl] = gl[c * chunk:c * chunk + 8]


def _gdn_scan_kernel(kwq_ref, n_ref, o0_ref, gl_ref, z_ref, nw_ref, o_ref, s_ref, *, hg):
    chunk, dh = GDN_CHUNK, GDN_HEAD_DIM
    nb, groups, n_chunks = kwq_ref.shape[:3]
    width = hg * dh
    chains = [(b, p) for b in range(nb) for p in range(groups)]

    @pl.when(pl.program_id(1) == 0)
    def _():
        s_ref[...] = jnp.zeros_like(s_ref)

    zero = jnp.zeros((dh, dh), BF16)

    def scan(n, carry):
        rows = pl.ds(pl.multiple_of(n * chunk, chunk), chunk)
        state = [s_ref[b, p] for b, p in chains]
        s16 = [st.astype(BF16) for st in state]
        s_bd = [jnp.concatenate([jnp.concatenate([x[:, :dh], zero], axis=1),
                                 jnp.concatenate([zero, x[:, dh:]], axis=1)], axis=0) for x in s16]
        both = [_dot(kwq_ref[b, p, n], s_bd[i]) for i, (b, p) in enumerate(chains)]
        for i, (b, p) in enumerate(chains):
            s_ref[b, p] = (state[i] * jnp.concatenate([gl_ref[b, p, n]] * (dh // 8), axis=0)
                           + n_ref[b, p, n].astype(F32) - both[i][:dh])
        for i, (b, p) in enumerate(chains):
            o = both[i][dh:] + o0_ref[b, rows, p * width:(p + 1) * width].astype(F32)
            for h in range(hg):
                hl = slice(p * width + h * dh, p * width + (h + 1) * dh)
                oh = o[:, h * dh:(h + 1) * dh]
                oh = oh * lax.rsqrt(jnp.mean(oh * oh, axis=-1, keepdims=True) + EPS) * nw_ref[...]
                zg = z_ref[b, rows, hl].astype(F32)
                o_ref[b, rows, hl] = (oh * (zg * jax.nn.sigmoid(zg))).astype(o_ref.dtype)
        return carry

    lax.fori_loop(0, n_chunks, scan, 0)


def _gdn(proj, gb, norm_w, n_batch, seq, n_heads):
    hg = 2
    dh, chunk = GDN_HEAD_DIM, GDN_CHUNK
    width = hg * dh
    groups = n_heads // hg
    mix = n_heads * dh
    rows = GDN_TILE * GDN_TILES_PER_STEP
    cps = rows // chunk
    n_chunks = seq // chunk
    col = lambda base: (lambda b, p, t: (b, t, base * groups + p))
    per_chunk = lambda r: pl.BlockSpec((1, 1, cps, r, width), lambda b, p, t: (b, p, t, 0, 0))
    kwq, n_op, o0, gl = pl.pallas_call(
        functools.partial(_gdn_ops_kernel, n_heads=n_heads, hg=hg),
        grid=(n_batch, groups, seq // rows),
        in_specs=[pl.BlockSpec((1, rows, width), col(0)), pl.BlockSpec((1, rows, width), col(1)),
                  pl.BlockSpec((1, rows, width), col(2)),
                  pl.BlockSpec((1, rows, LANES), lambda b, p, t: (b, t, 0))],
        out_specs=[per_chunk(dh + chunk), per_chunk(dh),
                   pl.BlockSpec((1, rows, width), lambda b, p, t: (b, t, p)), per_chunk(8)],
        out_shape=[jax.ShapeDtypeStruct((n_batch, groups, n_chunks, dh + chunk, width), BF16),
                   jax.ShapeDtypeStruct((n_batch, groups, n_chunks, dh, width), BF16),
                   jax.ShapeDtypeStruct((n_batch, seq, mix), BF16),
                   jax.ShapeDtypeStruct((n_batch, groups, n_chunks, 8, width), F32)],
        scratch_shapes=[pltpu.VMEM((GDN_TILES_PER_STEP, LANES, GDN_TILE), F32)],
        compiler_params=_params("parallel", "parallel", "parallel"),
        name="gdn_ops",
    )(proj, proj, proj, gb)
    nb = GDN_SCAN_BATCH if n_batch % GDN_SCAN_BATCH == 0 else 1
    srows = GDN_SCAN_ROWS
    scps = srows // chunk
    all_pairs = lambda r: pl.BlockSpec((nb, groups, scps, r, width), lambda b, t: (b, 0, t, 0, 0))
    return pl.pallas_call(
        functools.partial(_gdn_scan_kernel, hg=hg),
        grid=(n_batch // nb, seq // srows),
        in_specs=[all_pairs(dh + chunk), all_pairs(dh),
                  pl.BlockSpec((nb, srows, mix), lambda b, t: (b, t, 0)), all_pairs(8),
                  pl.BlockSpec((nb, srows, mix), lambda b, t: (b, t, 3)),
                  _const_spec((1, dh))],
        out_specs=pl.BlockSpec((nb, srows, mix), lambda b, t: (b, t, 0)),
        out_shape=jax.ShapeDtypeStruct((n_batch, seq, mix), BF16),
        scratch_shapes=[pltpu.VMEM((nb, groups, dh, width), F32)],
        compiler_params=_params("parallel", "arbitrary"),
        name="gdn_scan",
    )(kwq, n_op, o0, gl, proj, norm_w)


def kernel(x, p, mix_norm_e, w_in_e, pool_w, pool_scale, w_out_e, mix_norm_o, w_in_o, conv_qkv_o, a_log_o,
           dt_bias_o, gdn_norm_o, w_out_o, ffn_norm, w_up, ffn_conv, w_down, ple_norm, w_ple_gate, w_ple,
           final_norm):
    n_batch, seq, d = x.shape
    depth = p.shape[0]
    m = n_batch * seq
    pool_width = len(POOL_WINDOWS) * LANES
    row = lambda a: a.reshape(1, -1)
    xf = x.reshape(m, d)
    for i in range(depth):
        j = i // 2
        if i % 2 == 0:
            sb_width = (w_in_e.shape[2] - pool_width) // 3
            scale = jnp.concatenate([jnp.ones((pool_width,), F32),
                                     jnp.full((sb_width,), SB_HEAD_DIM ** -0.5, F32),
                                     jnp.ones((2 * sb_width,), F32)])
            proj = _norm_matmul(xf, row(mix_norm_e[j]), (w_in_e[j] * scale).astype(BF16))
            pc, sc = pool_width // LANES, sb_width // LANES
            attn = _sb_attention(proj.reshape(n_batch, seq, -1), n_batch, seq, pc, pc + sc, pc + 2 * sc, sc)
            mixer = ("even", proj, attn.reshape(m, sb_width), pool_w[j].astype(BF16), row(pool_scale[j]),
                     w_out_e[j].astype(BF16))
        else:
            n_heads = a_log_o.shape[1]
            mix = n_heads * GDN_HEAD_DIM
            wba = jnp.pad(w_in_o[j][:, 4 * mix:], ((0, 0), (0, LANES - 2 * n_heads))).astype(BF16)
            lane_pad = lambda a: jnp.pad(a, (n_heads, LANES - 2 * n_heads)).reshape(1, LANES)
            proj, gb = _odd_in_proj(xf, row(mix_norm_o[j]), w_in_o[j][:, :4 * mix].astype(BF16), conv_qkv_o[j],
                                    wba, lane_pad(a_log_o[j]), lane_pad(dt_bias_o[j]), n_heads, seq)
            o = _gdn(proj.reshape(n_batch, seq, -1), gb.reshape(n_batch, seq, LANES), row(gdn_norm_o[j]),
                     n_batch, seq, n_heads)
            mixer = ("odd", o.reshape(m, mix), w_out_o[j].astype(BF16))
        xf = _layer_tail(xf, mixer, (i, p.reshape(depth * m, -1)), row(ffn_norm[i]), w_up[i].astype(BF16), ffn_conv[i],
                         w_down[i].astype(BF16), row(ple_norm[i]), w_ple_gate[i].astype(BF16),
                         w_ple[i].astype(BF16), row(final_norm), seq, i == depth - 1)
    return xf.reshape(n_batch, seq, d)
```

```python
import functools

import jax
import jax.numpy as jnp
from jax import lax
from jax.experimental import pallas as pl
from jax.experimental.pallas import tpu as pltpu

F32 = jnp.float32
BF16 = jnp.bfloat16

EPS = 1e-6
LANES = 128
SUBLANES = 8
POOL_WINDOWS = (2, 4, 8, 16)
POOL_HALO = 16
SB_HEAD_DIM = 64
SB_TILE = 128
SB_GROUP = 8
SB_LOG_UNDERFLOW = 88.0
GDN_HEAD_DIM = 128
GDN_CHUNK = 64
GDN_TILE = 256
GDN_TILES_PER_STEP = 8
GDN_SCAN_BATCH = 2
GDN_SCAN_ROWS = 512
GDN_CONV = 4
GDN_HALO = 16
FFN_CONV = 3
CONV_HALO = 8
ROW_TILE = 512
FFN_CHUNK = 256
VMEM_LIMIT_BYTES = 56 * 1024 * 1024


def _params(*sem):
    return pltpu.CompilerParams(dimension_semantics=sem, vmem_limit_bytes=VMEM_LIMIT_BYTES)


def _const_spec(shape):
    nd = len(shape)
    return pl.BlockSpec(shape, lambda *_: (0,) * nd, pipeline_mode=pl.Buffered(1))


def _rms(x, gain):
    return x * lax.rsqrt(jnp.mean(x * x, axis=-1, keepdims=True) + EPS) * gain


def _dot(a, b):
    return jnp.dot(a, b, preferred_element_type=F32)


def _dot_nt(a, b):
    return lax.dot_general(a, b, (((1,), (1,)), ((), ())), preferred_element_type=F32)


def _dot_tn(a, b):
    return lax.dot_general(a, b, (((0,), (0,)), ((), ())), preferred_element_type=F32)


def _split_bf16(x):
    hi = x.astype(BF16)
    lo = (x - hi.astype(F32)).astype(BF16)
    return hi, lo


def _norm_matmul_kernel(x_ref, g_ref, w_ref, o_ref, *, n_chunk):
    xn = _rms(x_ref[...], g_ref[...]).astype(BF16)
    for c in range(0, w_ref.shape[1], n_chunk):
        o_ref[:, c:c + n_chunk] = _dot(xn, w_ref[:, c:c + n_chunk]).astype(o_ref.dtype)


def _norm_matmul(x, gain, w, *, tm=ROW_TILE, n_chunk=512):
    m, d = x.shape
    n = w.shape[1]
    return pl.pallas_call(
        functools.partial(_norm_matmul_kernel, n_chunk=n_chunk),
        grid=(m // tm,),
        in_specs=[pl.BlockSpec((tm, d), lambda r: (r, 0)), _const_spec((1, d)), _const_spec((d, n))],
        out_specs=pl.BlockSpec((tm, n), lambda r: (r, 0)),
        out_shape=jax.ShapeDtypeStruct((m, n), BF16),
        compiler_params=_params("parallel"),
        name="norm_matmul",
    )(x, gain, w)


def _odd_in_kernel(x_ref, g_ref, w_ref, cw_ref, wba_ref, alog_ref, dt_ref, o_ref, gb_ref, xs_ref, perm_ref,
                   carry_ref, *, n_chunk, n_heads, seq):
    tm = x_ref.shape[0]
    mix = n_heads * GDN_HEAD_DIM
    nb = tm // SUBLANES
    hist = (GDN_CONV - 1) * SUBLANES
    first = (pl.program_id(0) * tm) % seq == 0
    sub0 = lax.broadcasted_iota(jnp.int32, (SUBLANES, n_chunk), 0) == 0
    def scattered(a):
        return pl.ds((nb * (a % SUBLANES) + a // SUBLANES), SUBLANES, stride=SUBLANES)

    slabs = x_ref.shape[1] // LANES
    for j in range(slabs):
        for a in range(nb):
            xs_ref[j, scattered(a), :] = x_ref[a * SUBLANES:(a + 1) * SUBLANES, j * LANES:(j + 1) * LANES]
    xn = _rms(jnp.concatenate([xs_ref[j] for j in range(slabs)], axis=1), g_ref[...]).astype(BF16)

    def time_order(y):
        width = y.shape[1] // LANES
        for j in range(width):
            perm_ref[j] = y[:, j * LANES:(j + 1) * LANES]
        return jnp.concatenate(
            [jnp.concatenate([perm_ref[j, scattered(a), :] for j in range(width)], axis=1) for a in range(nb)],
            axis=0)

    for c in range(0, w_ref.shape[1], n_chunk):
        cs = slice(c, c + n_chunk)
        y = _dot(xn, w_ref[:, cs])
        if c < 3 * mix:
            prev = carry_ref[:, cs]
            prev = jnp.where(first, jnp.zeros_like(prev), prev)
            tail = y[tm - hist:, :]
            carry_ref[:, cs] = tail
            wrap = jnp.concatenate(
                [jnp.where(sub0, pltpu.roll(prev[b:b + SUBLANES], 1, axis=0),
                           pltpu.roll(tail[b:b + SUBLANES], 1, axis=0)) for b in range(0, hist, SUBLANES)], axis=0)
            w = cw_ref[:, cs]
            z = w[GDN_CONV - 1:GDN_CONV] * y
            for i in range(1, GDN_CONV):
                back = jnp.concatenate([wrap[hist - i * SUBLANES:], y[:tm - i * SUBLANES]], axis=0)
                z = z + w[GDN_CONV - 1 - i:GDN_CONV - i] * back
            y = z * jax.nn.sigmoid(z)
            if c < 2 * mix:
                scale = GDN_HEAD_DIM ** -0.5 if c < mix else 1.0
                heads = []
                for hc in range(0, n_chunk, GDN_HEAD_DIM):
                    yh = y[:, hc:hc + GDN_HEAD_DIM]
                    inv = lax.rsqrt(jnp.sum(yh * yh, axis=-1, keepdims=True) + EPS)
                    heads.append(yh * (inv * scale))
                y = jnp.concatenate(heads, axis=1)
        o_ref[:, cs] = time_order(y).astype(o_ref.dtype)
    ba = time_order(_dot(xn, wba_ref[...]))
    lane = lax.broadcasted_iota(jnp.int32, ba.shape, 1)
    is_g = jnp.logical_and(lane >= n_heads, lane < 2 * n_heads)
    g = jnp.where(is_g, -jnp.exp(alog_ref[...]) * jax.nn.softplus(ba + dt_ref[...]), 0.0)
    beta = jax.nn.sigmoid(ba)
    sub = GDN_TILE
    row = lax.broadcasted_iota(jnp.int32, (sub, sub), 0)
    col = lax.broadcasted_iota(jnp.int32, (sub, sub), 1)
    tri = jnp.where(jnp.logical_and(row // GDN_CHUNK == col // GDN_CHUNK, row >= col), 1.0, 0.0).astype(BF16)
    tri2 = jnp.concatenate([tri, tri], axis=1)
    is_beta = lax.broadcasted_iota(jnp.int32, (sub, LANES), 1) < n_heads
    for r in range(0, ba.shape[0], sub):
        hi, lo = _split_bf16(g[r:r + sub])
        gc = _dot(tri2, jnp.concatenate([hi, lo], axis=0))
        gb_ref[r:r + sub, :] = jnp.where(is_beta, beta[r:r + sub], gc)


def _odd_in_proj(x, gain, w, conv_w, wba, alog_lane, dt_lane, n_heads, seq, *, tm=ROW_TILE, n_chunk=512):
    m, d = x.shape
    n = w.shape[1]
    return pl.pallas_call(
        functools.partial(_odd_in_kernel, n_chunk=n_chunk, n_heads=n_heads, seq=seq),
        grid=(m // tm,),
        in_specs=[pl.BlockSpec((tm, d), lambda r: (r, 0)), _const_spec((1, d)), _const_spec((d, n)),
                  _const_spec(conv_w.shape), _const_spec((d, LANES)), _const_spec((1, LANES)),
                  _const_spec((1, LANES))],
        out_specs=[pl.BlockSpec((tm, n), lambda r: (r, 0)), pl.BlockSpec((tm, LANES), lambda r: (r, 0))],
        out_shape=[jax.ShapeDtypeStruct((m, n), BF16), jax.ShapeDtypeStruct((m, LANES), F32)],
        scratch_shapes=[pltpu.VMEM((d // LANES, tm, LANES), F32), pltpu.VMEM((n_chunk // LANES, tm, LANES), F32),
                        pltpu.VMEM(((GDN_CONV - 1) * SUBLANES, conv_w.shape[1]), F32)],
        compiler_params=_params("arbitrary"),
        name="odd_in_proj",
    )(x, gain, w, conv_w, wba, alog_lane, dt_lane)


def _sb_attn_kernel(q_ref, k_ref, v_ref, o_ref, acc_ref, c_ref):
    t = SB_TILE
    seq = q_ref.shape[1]
    lane = lax.broadcasted_iota(jnp.int32, (t, LANES), 1)
    head0 = lane < SB_HEAD_DIM
    row = lax.broadcasted_iota(jnp.int32, (t, t), 0)
    col = lax.broadcasted_iota(jnp.int32, (t, t), 1)
    r2 = lax.broadcasted_iota(jnp.int32, (2 * t, 2 * t), 0) % t
    c2 = lax.broadcasted_iota(jnp.int32, (2 * t, 2 * t), 1)
    cum = jnp.where((c2 >= t) | (r2 > c2), 1.0, 0.0).astype(BF16)

    below_diag = col < row

    def rows_of(i):
        return pl.ds(i * t, t) if isinstance(i, int) else pl.ds(pl.multiple_of(i * t, t), t)

    def split_heads(x):
        zero = jnp.zeros_like(x)
        return (jnp.where(head0, x, zero), jnp.where(head0, zero, x))

    def log_sigmoids(qm, kb):
        z = _dot_nt(qm, kb)
        ls_pos = jnp.minimum(z, 0.0) - jnp.log(1.0 + jnp.exp(-jnp.abs(z)))
        return ls_pos, ls_pos - z

    def running(ls_neg):
        hi, lo = _split_bf16(ls_neg)
        return _dot(jnp.concatenate([hi, lo], axis=1), cum)

    def sweep(i, slot, j_start, top_start):
        qh = split_heads(q_ref[0, rows_of(i), :])

        def cond(st):
            j, top = st
            return jnp.logical_and(j >= 0, top > -SB_LOG_UNDERFLOW)

        def body(st):
            j, _ = st
            kb = k_ref[0, rows_of(j), :]
            vh = split_heads(v_ref[0, rows_of(j), :])
            valid = (col - row) < (i - j) * t
            acc = acc_ref[slot]
            top = jnp.float32(-jnp.inf)
            for h in range(2):
                ls_pos, ls_neg = log_sigmoids(qh[h], kb)
                run = running(jnp.where(valid, ls_neg, 0.0))
                c_old = c_ref[slot, h]
                a = jnp.where(valid, jnp.exp(ls_pos + run[:, :t] + c_old), 0.0)
                acc = acc + _dot(a.astype(BF16), vh[h])
                c_new = c_old + run[:, t:]
                c_ref[slot, h] = c_new
                top = jnp.maximum(top, jnp.max(c_new))
            acc_ref[slot] = acc
            return j - 1, top

        lax.while_loop(cond, body, (j_start, top_start))
        o_ref[0, rows_of(i), :] = acc_ref[slot].astype(o_ref.dtype)

    def tile_group(first):
        tiles = [first + a for a in range(SB_GROUP)]
        blocks = [[0] if isinstance(i, int) and i == 0 else [0, 1] for i in tiles]
        streams = [(a, b, h) for a in range(SB_GROUP) for b in blocks[a] for h in range(2)]
        qh = [split_heads(q_ref[0, rows_of(i), :]) for i in tiles]
        kb = [[k_ref[0, rows_of(i - b), :] for b in blocks[a]] for a, i in enumerate(tiles)]
        vh = [[split_heads(v_ref[0, rows_of(i - b), :]) for b in blocks[a]] for a, i in enumerate(tiles)]
        ls = {s: log_sigmoids(qh[s[0]][s[2]], kb[s[0]][s[1]]) for s in streams}
        run = {s: running(jnp.where(below_diag, ls[s][1], 0.0) if s[1] == 0 else ls[s][1]) for s in streams}
        tops = []
        for a in range(SB_GROUP):
            acc = jnp.zeros((t, LANES), F32)
            top = jnp.float32(-jnp.inf)
            for h in range(2):
                d = (a, 0, h)
                weights = [jnp.where(below_diag, jnp.exp(ls[d][0] + run[d][:, :t]), 0.0)]
                c_new = run[d][:, t:]
                if len(blocks[a]) == 2:
                    l = (a, 1, h)
                    weights.append(jnp.exp(ls[l][0] + run[l][:, :t] + c_new))
                    c_new = c_new + run[l][:, t:]
                acc = acc + _dot(jnp.concatenate(weights, axis=1).astype(BF16),
                                 jnp.concatenate([vh[a][b][h] for b in blocks[a]], axis=0))
                c_ref[a, h] = c_new
                top = jnp.maximum(top, jnp.max(c_new))
            acc_ref[a] = acc
            tops.append(top)
        for a in range(SB_GROUP):
            sweep(tiles[a], a, jnp.int32(tiles[a] - 2), tops[a])

    tile_group(0)

    def later_group(g, carry):
        tile_group(g * SB_GROUP)
        return carry

    lax.fori_loop(1, seq // (SB_GROUP * t), later_group, 0)


def _sb_attention(proj, n_batch, seq, q_col, k_col, v_col, n_pairs):
    blk = (1, seq, LANES)
    return pl.pallas_call(
        _sb_attn_kernel,
        grid=(n_batch, n_pairs),
        in_specs=[pl.BlockSpec(blk, lambda b, p: (b, 0, q_col + p)),
                  pl.BlockSpec(blk, lambda b, p: (b, 0, k_col + p)),
                  pl.BlockSpec(blk, lambda b, p: (b, 0, v_col + p))],
        out_specs=pl.BlockSpec(blk, lambda b, p: (b, 0, p)),
        out_shape=jax.ShapeDtypeStruct((n_batch, seq, n_pairs * LANES), BF16),
        scratch_shapes=[pltpu.VMEM((SB_GROUP, SB_TILE, LANES), F32),
                        pltpu.VMEM((SB_GROUP, 2, SB_TILE, LANES), F32)],
        compiler_params=_params("parallel", "parallel"),
        name="sb_attention",
    )(proj, proj, proj)


def _pool_mixer_out(u_ref, halo_ref, attn_ref, pw_ref, ps_ref, wo_ref, s_ref, seq):
    tm = u_ref.shape[0]
    t0 = (pl.program_id(0) * tm) % seq
    u = u_ref[...].astype(F32)
    halo = halo_ref[...].astype(F32)
    s_ref[0:POOL_HALO, :] = jnp.where(t0 == 0, jnp.zeros_like(halo), halo)
    s_ref[POOL_HALO:, :] = u
    tpos = t0 + lax.broadcasted_iota(jnp.int32, (tm, 1), 0)
    parts = []
    for g, win in enumerate(POOL_WINDOWS):
        cs = slice(g * LANES, (g + 1) * LANES)
        ug = u[:, cs]
        ws = ug
        for k in range(1, win):
            ws = ws + s_ref[POOL_HALO - k:POOL_HALO - k + tm, cs]
        count = jnp.minimum(tpos + 1, win).astype(F32)
        y = ws / count - ug
        parts.append((_dot(y.astype(BF16), pw_ref[g]) * ps_ref[:, cs]).astype(BF16))
    cat = jnp.concatenate(parts + [attn_ref[...]], axis=1)
    return _dot(cat, wo_ref[...])


def _ffn_ple_tail(x, p_ref, fg_ref, wup_ref, cw_ref, wdn_ref, pg_ref, wg_ref, wp_ref, fn_ref, o_ref,
                  h_ref, ybuf_ref, carry_ref, *, seq, tf, final_norm):
    tm = x.shape[0]
    ffn = wdn_ref.shape[0]
    first = (pl.program_id(0) * tm) % seq == 0
    xn = _rms(x, fg_ref[...]).astype(BF16)
    for c in range(ffn // tf):
        conv = []
        for part in range(2):
            cs = slice(part * ffn + c * tf, part * ffn + (c + 1) * tf)
            y = _dot(xn, wup_ref[:, cs])
            prev = carry_ref[:, cs]
            ybuf_ref[part, 0:CONV_HALO, :] = jnp.where(first, jnp.zeros_like(prev), prev)
            ybuf_ref[part, CONV_HALO:, :] = y
            carry_ref[:, cs] = y[tm - CONV_HALO:, :]
            w = cw_ref[:, cs]
            conv.append(w[2:3] * y
                        + w[1:2] * ybuf_ref[part, CONV_HALO - 1:CONV_HALO - 1 + tm, :]
                        + w[0:1] * ybuf_ref[part, CONV_HALO - 2:CONV_HALO - 2 + tm, :])
        h_ref[:, c * tf:(c + 1) * tf] = (conv[0] * jax.nn.sigmoid(conv[0]) * conv[1]).astype(BF16)
    x2 = x + _dot(h_ref[...], wdn_ref[...])
    gate = jax.nn.sigmoid(_dot(_rms(x2, pg_ref[...]).astype(BF16), wg_ref[...]))
    x3 = x2 + _dot(p_ref[...].astype(BF16), wp_ref[...]) * gate
    if final_norm:
        x3 = _rms(x3, fn_ref[...])
    o_ref[...] = x3


def _even_tail_kernel(x_ref, u_ref, halo_ref, attn_ref, pw_ref, ps_ref, wo_ref, *rest, seq, tf, final_norm):
    *ffn_refs, s_ref, h_ref, ybuf_ref, carry_ref = rest
    x = x_ref[...] + _pool_mixer_out(u_ref, halo_ref, attn_ref, pw_ref, ps_ref, wo_ref, s_ref, seq)
    _ffn_ple_tail(x, *ffn_refs, h_ref, ybuf_ref, carry_ref, seq=seq, tf=tf, final_norm=final_norm)


def _odd_tail_kernel(x_ref, a_ref, wo_ref, *rest, seq, tf, final_norm):
    x = x_ref[...] + _dot(a_ref[...], wo_ref[...])
    _ffn_ple_tail(x, *rest, seq=seq, tf=tf, final_norm=final_norm)


def _layer_tail(x, mixer, p, ffn_gain, w_up, conv_w, w_down, ple_gain, w_gate, w_ple, final_gain, seq, final_norm,
                *, tm=ROW_TILE, tf=FFN_CHUNK):
    m, d = x.shape
    row = lambda r: (r, 0)
    layer, p = p
    p_row = lambda r: (layer * (m // tm) + r, 0)
    scratch = [pltpu.VMEM((tm, w_down.shape[0]), BF16), pltpu.VMEM((2, tm + CONV_HALO, tf), F32),
               pltpu.VMEM((CONV_HALO, w_up.shape[1]), F32)]
    if mixer[0] == "even":
        _, proj, attn, pool_w, pool_scale, w_out = mixer
        pw = len(POOL_WINDOWS) * LANES
        hb = tm // POOL_HALO
        body = _even_tail_kernel
        mixer_args = (proj, proj, attn, pool_w, pool_scale, w_out)
        mixer_specs = [pl.BlockSpec((tm, pw), row),
                       pl.BlockSpec((POOL_HALO, pw), lambda r: (jnp.maximum(r * hb - 1, 0), 0)),
                       pl.BlockSpec((tm, attn.shape[1]), row),
                       _const_spec(pool_w.shape), _const_spec(pool_scale.shape), _const_spec(w_out.shape)]
        scratch = [pltpu.VMEM((tm + POOL_HALO, pw), F32)] + scratch
    else:
        _, o, w_out = mixer
        body = _odd_tail_kernel
        mixer_args = (o, w_out)
        mixer_specs = [pl.BlockSpec((tm, o.shape[1]), row), _const_spec(w_out.shape)]
    return pl.pallas_call(
        functools.partial(body, seq=seq, tf=tf, final_norm=final_norm),
        grid=(m // tm,),
        in_specs=[pl.BlockSpec((tm, d), row)] + mixer_specs
                 + [pl.BlockSpec((tm, p.shape[1]), p_row),
                    _const_spec((1, d)), _const_spec(w_up.shape), _const_spec(conv_w.shape),
                    _const_spec(w_down.shape), _const_spec((1, d)), _const_spec(w_gate.shape),
                    _const_spec(w_ple.shape), _const_spec((1, d))],
        out_specs=pl.BlockSpec((tm, d), row),
        out_shape=jax.ShapeDtypeStruct((m, d), F32),
        scratch_shapes=scratch,
        compiler_params=_params("arbitrary"),
        name=mixer[0] + "_tail",
    )(x, *mixer_args, p, ffn_gain, w_up, conv_w, w_down, ple_gain, w_gate, w_ple, final_gain)


def _gdn_ops_kernel(q_ref, k_ref, v_ref, gb_ref, kwq_ref, n_ref, o0_ref, gl_ref, tr_ref, *, n_heads, hg):
    tile, chunk, dh = GDN_TILE, GDN_CHUNK, GDN_HEAD_DIM
    cpt = tile // chunk
    row = lax.broadcasted_iota(jnp.int32, (tile, tile), 0)
    col = lax.broadcasted_iota(jnp.int32, (tile, tile), 1)
    same = (row // chunk) == (col // chunk)
    incl = jnp.logical_and(same, row >= col)
    strict = jnp.logical_and(same, row > col)
    lane = lax.broadcasted_iota(jnp.int32, (tile, LANES), 1)

    def block_diag(wide):
        return jnp.where(same, jnp.concatenate([wide] * cpt, axis=0), 0.0).astype(BF16)

    streams = [(dt, h) for dt in range(GDN_TILES_PER_STEP) for h in range(hg)]
    ns = range(len(streams))
    hls = [slice(h * dh, (h + 1) * dh) for _, h in streams]
    heads = [pl.program_id(1) * hg + h for _, h in streams]

    r0s = [dt * tile for dt, _ in streams]
    gbts = [gb_ref[0, r0:r0 + tile, :] for r0 in r0s]
    for s in ns:
        if streams[s][1] == 0:
            tr_ref[streams[s][0]] = gbts[s].T
    q = [q_ref[0, r0s[s]:r0s[s] + tile, hls[s]].astype(F32) for s in ns]
    k16 = [k_ref[0, r0s[s]:r0s[s] + tile, hls[s]] for s in ns]
    k = [k16[s].astype(F32) for s in ns]
    v = [v_ref[0, r0s[s]:r0s[s] + tile, hls[s]].astype(F32) for s in ns]
    beta = [jnp.sum(jnp.where(lane == heads[s], gbts[s], 0.0), axis=-1, keepdims=True) for s in ns]
    gc = [jnp.sum(jnp.where(lane == n_heads + heads[s], gbts[s], 0.0), axis=-1, keepdims=True) for s in ns]
    gc_b = [jnp.broadcast_to(gc[s], (tile, dh)) for s in ns]
    gc_last = [jnp.broadcast_to(gc_b[s].reshape(cpt, chunk, dh)[:, chunk - 1:chunk, :],
                                (cpt, chunk, dh)).reshape(tile, dh) for s in ns]
    gc_row = [tr_ref[streams[s][0], pl.ds(n_heads + heads[s], 1), :] for s in ns]
    decay = [jnp.where(incl, jnp.exp(jnp.where(incl, gc[s] - gc_row[s], 0.0)), 0.0) for s in ns]
    k_beta = [k[s] * beta[s] for s in ns]
    kq = [_dot_nt(jnp.concatenate([k_beta[s].astype(BF16), q[s].astype(BF16)], axis=0), k16[s]) for s in ns]
    a_bd = [jnp.where(strict, kq[s][:tile] * decay[s], 0.0) for s in ns]
    qk16 = [(kq[s][tile:] * decay[s]).astype(BF16) for s in ns]
    pw = [-(a_bd[s][0:chunk] + a_bd[s][chunk:2 * chunk] + a_bd[s][2 * chunk:3 * chunk] + a_bd[s][3 * chunk:])
          for s in ns]
    res = list(pw)
    pw = [_dot(pw[s].astype(BF16), block_diag(pw[s])) for s in ns]
    step = 2
    while step < chunk:
        both = [_dot(jnp.concatenate([res[s], pw[s]], axis=0).astype(BF16), block_diag(pw[s])) for s in ns]
        res = [res[s] + pw[s] + both[s][:chunk] for s in ns]
        pw = [both[s][chunk:] for s in ns]
        step *= 2
    egc = [jnp.exp(gc_b[s]) for s in ns]
    rhs = [jnp.concatenate([v[s] * beta[s], k_beta[s] * egc[s]], axis=1) for s in ns]
    sol = [rhs[s] + _dot(block_diag(res[s]), rhs[s].astype(BF16)) for s in ns]
    sol16 = [sol[s].astype(BF16) for s in ns]
    qkuw = [_dot(qk16[s], sol16[s]) for s in ns]
    kd16 = [(k[s] * jnp.exp(gc_last[s] - gc_b[s])).astype(BF16) for s in ns]
    for s in ns:
        hl = hls[s]
        o0_ref[0, r0s[s]:r0s[s] + tile, hl] = qkuw[s][:, :dh].astype(o0_ref.dtype)
        qe16 = (q[s] * egc[s] - qkuw[s][:, dh:]).astype(BF16)
        gl = jnp.exp(gc_last[s])
        for c in range(cpt):
            cr = slice(c * chunk, (c + 1) * chunk)
            ci = streams[s][0] * cpt + c
            nkw = _dot_tn(kd16[s][cr], sol16[s][cr])
            n_ref[0, 0, ci, :, hl] = nkw[:, :dh].astype(n_ref.dtype)
            kwq_ref[0, 0, ci, 0:dh, hl] = nkw[:, dh:].astype(BF16)
            kwq_ref[0, 0, ci, dh:, hl] = qe16[cr]
            gl_ref[0, 0, ci, :, hl] = gl[c * chunk:c * chunk + 8]


def _gdn_scan_kernel(kwq_ref, n_ref, o0_ref, gl_ref, z_ref, nw_ref, o_ref, s_ref, *, hg):
    chunk, dh = GDN_CHUNK, GDN_HEAD_DIM
    nb, groups, n_chunks = kwq_ref.shape[:3]
    width = hg * dh
    chains = [(b, p) for b in range(nb) for p in range(groups)]

    @pl.when(pl.program_id(1) == 0)
    def _():
        s_ref[...] = jnp.zeros_like(s_ref)

    zero = jnp.zeros((dh, dh), BF16)

    def scan(n, carry):
        rows = pl.ds(pl.multiple_of(n * chunk, chunk), chunk)
        state = [s_ref[b, p] for b, p in chains]
        s16 = [st.astype(BF16) for st in state]
        s_bd = [jnp.concatenate([jnp.concatenate([x[:, :dh], zero], axis=1),
                                 jnp.concatenate([zero, x[:, dh:]], axis=1)], axis=0) for x in s16]
        both = [_dot(kwq_ref[b, p, n], s_bd[i]) for i, (b, p) in enumerate(chains)]
        for i, (b, p) in enumerate(chains):
            s_ref[b, p] = (state[i] * jnp.concatenate([gl_ref[b, p, n]] * (dh // 8), axis=0)
                           + n_ref[b, p, n].astype(F32) - both[i][:dh])
        for i, (b, p) in enumerate(chains):
            o = both[i][dh:] + o0_ref[b, rows, p * width:(p + 1) * width].astype(F32)
            for h in range(hg):
                hl = slice(p * width + h * dh, p * width + (h + 1) * dh)
                oh = o[:, h * dh:(h + 1) * dh]
                oh = oh * lax.rsqrt(jnp.mean(oh * oh, axis=-1, keepdims=True) + EPS) * nw_ref[...]
                zg = z_ref[b, rows, hl].astype(F32)
                o_ref[b, rows, hl] = (oh * (zg * jax.nn.sigmoid(zg))).astype(o_ref.dtype)
        return carry

    lax.fori_loop(0, n_chunks, scan, 0)


def _gdn(proj, gb, norm_w, n_batch, seq, n_heads):
    hg = 2
    dh, chunk = GDN_HEAD_DIM, GDN_CHUNK
    width = hg * dh
    groups = n_heads // hg
    mix = n_heads * dh
    rows = GDN_TILE * GDN_TILES_PER_STEP
    cps = rows // chunk
    n_chunks = seq // chunk
    col = lambda base: (lambda b, p, t: (b, t, base * groups + p))
    per_chunk = lambda r: pl.BlockSpec((1, 1, cps, r, width), lambda b, p, t: (b, p, t, 0, 0))
    kwq, n_op, o0, gl = pl.pallas_call(
        functools.partial(_gdn_ops_kernel, n_heads=n_heads, hg=hg),
        grid=(n_batch, groups, seq // rows),
        in_specs=[pl.BlockSpec((1, rows, width), col(0)), pl.BlockSpec((1, rows, width), col(1)),
                  pl.BlockSpec((1, rows, width), col(2)),
                  pl.BlockSpec((1, rows, LANES), lambda b, p, t: (b, t, 0))],
        out_specs=[per_chunk(dh + chunk), per_chunk(dh),
                   pl.BlockSpec((1, rows, width), lambda b, p, t: (b, t, p)), per_chunk(8)],
        out_shape=[jax.ShapeDtypeStruct((n_batch, groups, n_chunks, dh + chunk, width), BF16),
                   jax.ShapeDtypeStruct((n_batch, groups, n_chunks, dh, width), BF16),
                   jax.ShapeDtypeStruct((n_batch, seq, mix), BF16),
                   jax.ShapeDtypeStruct((n_batch, groups, n_chunks, 8, width), F32)],
        scratch_shapes=[pltpu.VMEM((GDN_TILES_PER_STEP, LANES, GDN_TILE), F32)],
        compiler_params=_params("parallel", "parallel", "parallel"),
        name="gdn_ops",
    )(proj, proj, proj, gb)
    nb = GDN_SCAN_BATCH if n_batch % GDN_SCAN_BATCH == 0 else 1
    srows = GDN_SCAN_ROWS
    scps = srows // chunk
    all_pairs = lambda r: pl.BlockSpec((nb, groups, scps, r, width), lambda b, t: (b, 0, t, 0, 0))
    return pl.pallas_call(
        functools.partial(_gdn_scan_kernel, hg=hg),
        grid=(n_batch // nb, seq // srows),
        in_specs=[all_pairs(dh + chunk), all_pairs(dh),
                  pl.BlockSpec((nb, srows, mix), lambda b, t: (b, t, 0)), all_pairs(8),
                  pl.BlockSpec((nb, srows, mix), lambda b, t: (b, t, 3)),
                  _const_spec((1, dh))],
        out_specs=pl.BlockSpec((nb, srows, mix), lambda b, t: (b, t, 0)),
        out_shape=jax.ShapeDtypeStruct((n_batch, seq, mix), BF16),
        scratch_shapes=[pltpu.VMEM((nb, groups, dh, width), F32)],
        compiler_params=_params("parallel", "arbitrary"),
        name="gdn_scan",
    )(kwq, n_op, o0, gl, proj, norm_w)


def kernel(x, p, mix_norm_e, w_in_e, pool_w, pool_scale, w_out_e, mix_norm_o, w_in_o, conv_qkv_o, a_log_o,
           dt_bias_o, gdn_norm_o, w_out_o, ffn_norm, w_up, ffn_conv, w_down, ple_norm, w_ple_gate, w_ple,
           final_norm):
    n_batch, seq, d = x.shape
    depth = p.shape[0]
    m = n_batch * seq
    pool_width = len(POOL_WINDOWS) * LANES
    row = lambda a: a.reshape(1, -1)
    xf = x.reshape(m, d)
    for i in range(depth):
        j = i // 2
        if i % 2 == 0:
            sb_width = (w_in_e.shape[2] - pool_width) // 3
            scale = jnp.concatenate([jnp.ones((pool_width,), F32),
                                     jnp.full((sb_width,), SB_HEAD_DIM ** -0.5, F32),
                                     jnp.ones((2 * sb_width,), F32)])
            proj = _norm_matmul(xf, row(mix_norm_e[j]), (w_in_e[j] * scale).astype(BF16))
            pc, sc = pool_width // LANES, sb_width // LANES
            attn = _sb_attention(proj.reshape(n_batch, seq, -1), n_batch, seq, pc, pc + sc, pc + 2 * sc, sc)
            mixer = ("even", proj, attn.reshape(m, sb_width), pool_w[j].astype(BF16), row(pool_scale[j]),
                     w_out_e[j].astype(BF16))
        else:
            n_heads = a_log_o.shape[1]
            mix = n_heads * GDN_HEAD_DIM
            wba = jnp.pad(w_in_o[j][:, 4 * mix:], ((0, 0), (0, LANES - 2 * n_heads))).astype(BF16)
            lane_pad = lambda a: jnp.pad(a, (n_heads, LANES - 2 * n_heads)).reshape(1, LANES)
            proj, gb = _odd_in_proj(xf, row(mix_norm_o[j]), w_in_o[j][:, :4 * mix].astype(BF16), conv_qkv_o[j],
                                    wba, lane_pad(a_log_o[j]), lane_pad(dt_bias_o[j]), n_heads, seq)
            o = _gdn(proj.reshape(n_batch, seq, -1), gb.reshape(n_batch, seq, LANES), row(gdn_norm_o[j]),
                     n_batch, seq, n_heads)
            mixer = ("odd", o.reshape(m, mix), w_out_o[j].astype(BF16))
        xf = _layer_tail(xf, mixer, (i, p.reshape(depth * m, -1)), row(ffn_norm[i]), w_up[i].astype(BF16), ffn_conv[i],
                         w_down[i].astype(BF16), row(ple_norm[i]), w_ple_gate[i].astype(BF16),
                         w_ple[i].astype(BF16), row(final_norm), seq, i == depth - 1)
    return xf.reshape(n_batch, seq, d)
```

```python
import functools

import jax
import jax.numpy as jnp
from jax import lax
from jax.experimental import pallas as pl
from jax.experimental.pallas import tpu as pltpu

F32 = jnp.float32
BF16 = jnp.bfloat16

EPS = 1e-6
LANES = 128
SUBLANES = 8
POOL_WINDOWS = (2, 4, 8, 16)
POOL_HALO = 16
SB_HEAD_DIM = 64
SB_TILE = 128
SB_GROUP = 8
SB_LOG_UNDERFLOW = 88.0
GDN_HEAD_DIM = 128
GDN_CHUNK = 64
GDN_TILE = 256
GDN_TILES_PER_STEP = 8
GDN_SCAN_BATCH = 2
GDN_SCAN_ROWS = 512
GDN_CONV = 4
FFN_CONV = 3
CONV_HALO = SUBLANES
ROW_TILE = 512
FFN_CHUNK = 256
VMEM_LIMIT_BYTES = 56 * 1024 * 1024


def _params(*sem):
    return pltpu.CompilerParams(dimension_semantics=sem, vmem_limit_bytes=VMEM_LIMIT_BYTES)


def _const_spec(shape):
    nd = len(shape)
    return pl.BlockSpec(shape, lambda *_: (0,) * nd, pipeline_mode=pl.Buffered(1))


def _rms(x, gain):
    return x * lax.rsqrt(jnp.mean(x * x, axis=-1, keepdims=True) + EPS) * gain


def _dot(a, b):
    return jnp.dot(a, b, preferred_element_type=F32)


def _dot_nt(a, b):
    return lax.dot_general(a, b, (((1,), (1,)), ((), ())), preferred_element_type=F32)


def _dot_tn(a, b):
    return lax.dot_general(a, b, (((0,), (0,)), ((), ())), preferred_element_type=F32)


def _split_bf16(x):
    hi = x.astype(BF16)
    lo = (x - hi.astype(F32)).astype(BF16)
    return hi, lo


def _norm_matmul_kernel(x_ref, g_ref, w_ref, o_ref, *, n_chunk):
    xn = _rms(x_ref[...], g_ref[...]).astype(BF16)
    for c in range(0, w_ref.shape[1], n_chunk):
        o_ref[:, c:c + n_chunk] = _dot(xn, w_ref[:, c:c + n_chunk]).astype(o_ref.dtype)


def _norm_matmul(x, gain, w, *, tm=ROW_TILE, n_chunk=512):
    m, d = x.shape
    n = w.shape[1]
    return pl.pallas_call(
        functools.partial(_norm_matmul_kernel, n_chunk=n_chunk),
        grid=(m // tm,),
        in_specs=[pl.BlockSpec((tm, d), lambda r: (r, 0)), _const_spec((1, d)), _const_spec((d, n))],
        out_specs=pl.BlockSpec((tm, n), lambda r: (r, 0)),
        out_shape=jax.ShapeDtypeStruct((m, n), BF16),
        compiler_params=_params("parallel"),
        name="norm_matmul",
    )(x, gain, w)


def _odd_in_kernel(x_ref, g_ref, w_ref, cw_ref, wba_ref, alog_ref, dt_ref, o_ref, gb_ref, xs_ref, perm_ref,
                   carry_ref, *, n_chunk, n_heads, seq):
    tm = x_ref.shape[0]
    mix = n_heads * GDN_HEAD_DIM
    nb = tm // SUBLANES
    hist = (GDN_CONV - 1) * SUBLANES
    first = (pl.program_id(0) * tm) % seq == 0
    sub0 = lax.broadcasted_iota(jnp.int32, (SUBLANES, n_chunk), 0) == 0

    def scattered(a):
        return pl.ds((nb * (a % SUBLANES) + a // SUBLANES), SUBLANES, stride=SUBLANES)

    slabs = x_ref.shape[1] // LANES
    for j in range(slabs):
        for a in range(nb):
            xs_ref[j, scattered(a), :] = x_ref[a * SUBLANES:(a + 1) * SUBLANES, j * LANES:(j + 1) * LANES]
    xn = _rms(jnp.concatenate([xs_ref[j] for j in range(slabs)], axis=1), g_ref[...]).astype(BF16)

    def time_order(y):
        width = y.shape[1] // LANES
        for j in range(width):
            perm_ref[j] = y[:, j * LANES:(j + 1) * LANES]
        return jnp.concatenate(
            [jnp.concatenate([perm_ref[j, scattered(a), :] for j in range(width)], axis=1) for a in range(nb)],
            axis=0)

    for c in range(0, w_ref.shape[1], n_chunk):
        cs = slice(c, c + n_chunk)
        y = _dot(xn, w_ref[:, cs])
        if c < 3 * mix:
            prev = carry_ref[:, cs]
            prev = jnp.where(first, jnp.zeros_like(prev), prev)
            tail = y[tm - hist:, :]
            carry_ref[:, cs] = tail
            wrap = jnp.concatenate(
                [jnp.where(sub0, pltpu.roll(prev[b:b + SUBLANES], 1, axis=0),
                           pltpu.roll(tail[b:b + SUBLANES], 1, axis=0)) for b in range(0, hist, SUBLANES)], axis=0)
            w = cw_ref[:, cs]
            z = w[GDN_CONV - 1:GDN_CONV] * y
            for i in range(1, GDN_CONV):
                back = jnp.concatenate([wrap[hist - i * SUBLANES:], y[:tm - i * SUBLANES]], axis=0)
                z = z + w[GDN_CONV - 1 - i:GDN_CONV - i] * back
            y = z * jax.nn.sigmoid(z)
            if c < 2 * mix:
                scale = GDN_HEAD_DIM ** -0.5 if c < mix else 1.0
                heads = []
                for hc in range(0, n_chunk, GDN_HEAD_DIM):
                    yh = y[:, hc:hc + GDN_HEAD_DIM]
                    inv = lax.rsqrt(jnp.sum(yh * yh, axis=-1, keepdims=True) + EPS)
                    heads.append(yh * (inv * scale))
                y = jnp.concatenate(heads, axis=1)
        o_ref[:, cs] = time_order(y).astype(o_ref.dtype)
    ba = time_order(_dot(xn, wba_ref[...]))
    lane = lax.broadcasted_iota(jnp.int32, ba.shape, 1)
    is_g = jnp.logical_and(lane >= n_heads, lane < 2 * n_heads)
    g = jnp.where(is_g, -jnp.exp(alog_ref[...]) * jax.nn.softplus(ba + dt_ref[...]), 0.0)
    beta = jax.nn.sigmoid(ba)
    sub = GDN_TILE
    row = lax.broadcasted_iota(jnp.int32, (sub, sub), 0)
    col = lax.broadcasted_iota(jnp.int32, (sub, sub), 1)
    tri = jnp.where(jnp.logical_and(row // GDN_CHUNK == col // GDN_CHUNK, row >= col), 1.0, 0.0).astype(BF16)
    tri2 = jnp.concatenate([tri, tri], axis=1)
    is_beta = lax.broadcasted_iota(jnp.int32, (sub, LANES), 1) < n_heads
    for r in range(0, ba.shape[0], sub):
        hi, lo = _split_bf16(g[r:r + sub])
        gc = _dot(tri2, jnp.concatenate([hi, lo], axis=0))
        gb_ref[r:r + sub, :] = jnp.where(is_beta, beta[r:r + sub], gc)


def _odd_in_proj(x, gain, w, conv_w, wba, alog_lane, dt_lane, n_heads, seq, *, tm=ROW_TILE, n_chunk=512):
    m, d = x.shape
    n = w.shape[1]
    return pl.pallas_call(
        functools.partial(_odd_in_kernel, n_chunk=n_chunk, n_heads=n_heads, seq=seq),
        grid=(m // tm,),
        in_specs=[pl.BlockSpec((tm, d), lambda r: (r, 0)), _const_spec((1, d)), _const_spec((d, n)),
                  _const_spec(conv_w.shape), _const_spec((d, LANES)), _const_spec((1, LANES)),
                  _const_spec((1, LANES))],
        out_specs=[pl.BlockSpec((tm, n), lambda r: (r, 0)), pl.BlockSpec((tm, LANES), lambda r: (r, 0))],
        out_shape=[jax.ShapeDtypeStruct((m, n), BF16), jax.ShapeDtypeStruct((m, LANES), F32)],
        scratch_shapes=[pltpu.VMEM((d // LANES, tm, LANES), F32), pltpu.VMEM((n_chunk // LANES, tm, LANES), F32),
                        pltpu.VMEM(((GDN_CONV - 1) * SUBLANES, conv_w.shape[1]), F32)],
        compiler_params=_params("arbitrary"),
        name="odd_in_proj",
    )(x, gain, w, conv_w, wba, alog_lane, dt_lane)


def _sb_attn_kernel(q_ref, k_ref, v_ref, o_ref, acc_ref, c_ref):
    t = SB_TILE
    seq = q_ref.shape[1]
    lane = lax.broadcasted_iota(jnp.int32, (t, LANES), 1)
    head0 = lane < SB_HEAD_DIM
    row = lax.broadcasted_iota(jnp.int32, (t, t), 0)
    col = lax.broadcasted_iota(jnp.int32, (t, t), 1)
    r2 = lax.broadcasted_iota(jnp.int32, (2 * t, 2 * t), 0) % t
    c2 = lax.broadcasted_iota(jnp.int32, (2 * t, 2 * t), 1)
    cum = jnp.where((c2 >= t) | (r2 > c2), 1.0, 0.0).astype(BF16)

    below_diag = col < row

    def rows_of(i):
        return pl.ds(i * t, t) if isinstance(i, int) else pl.ds(pl.multiple_of(i * t, t), t)

    def split_heads(x):
        zero = jnp.zeros_like(x)
        return (jnp.where(head0, x, zero), jnp.where(head0, zero, x))

    def log_sigmoids(qm, kb):
        z = _dot_nt(qm, kb)
        ls_pos = jnp.minimum(z, 0.0) - jnp.log(1.0 + jnp.exp(-jnp.abs(z)))
        return ls_pos, ls_pos - z

    def running(ls_neg):
        hi, lo = _split_bf16(ls_neg)
        return _dot(jnp.concatenate([hi, lo], axis=1), cum)

    def sweep(i, slot, j_start, top_start):
        qh = split_heads(q_ref[0, rows_of(i), :])

        def cond(st):
            j, top = st
            return jnp.logical_and(j >= 0, top > -SB_LOG_UNDERFLOW)

        def body(st):
            j, _ = st
            kb = k_ref[0, rows_of(j), :]
            vh = split_heads(v_ref[0, rows_of(j), :])
            valid = (col - row) < (i - j) * t
            acc = acc_ref[slot]
            top = jnp.float32(-jnp.inf)
            for h in range(2):
                ls_pos, ls_neg = log_sigmoids(qh[h], kb)
                run = running(jnp.where(valid, ls_neg, 0.0))
                c_old = c_ref[slot, h]
                a = jnp.where(valid, jnp.exp(ls_pos + run[:, :t] + c_old), 0.0)
                acc = acc + _dot(a.astype(BF16), vh[h])
                c_new = c_old + run[:, t:]
                c_ref[slot, h] = c_new
                top = jnp.maximum(top, jnp.max(c_new))
            acc_ref[slot] = acc
            return j - 1, top

        lax.while_loop(cond, body, (j_start, top_start))
        o_ref[0, rows_of(i), :] = acc_ref[slot].astype(o_ref.dtype)

    def tile_group(first):
        tiles = [first + a for a in range(SB_GROUP)]
        blocks = [[0] if isinstance(i, int) and i == 0 else [0, 1] for i in tiles]
        streams = [(a, b, h) for a in range(SB_GROUP) for b in blocks[a] for h in range(2)]
        qh = [split_heads(q_ref[0, rows_of(i), :]) for i in tiles]
        kb = [[k_ref[0, rows_of(i - b), :] for b in blocks[a]] for a, i in enumerate(tiles)]
        vh = [[split_heads(v_ref[0, rows_of(i - b), :]) for b in blocks[a]] for a, i in enumerate(tiles)]
        ls = {s: log_sigmoids(qh[s[0]][s[2]], kb[s[0]][s[1]]) for s in streams}
        run = {s: running(jnp.where(below_diag, ls[s][1], 0.0) if s[1] == 0 else ls[s][1]) for s in streams}
        tops = []
        for a in range(SB_GROUP):
            acc = jnp.zeros((t, LANES), F32)
            top = jnp.float32(-jnp.inf)
            for h in range(2):
                d = (a, 0, h)
                weights = [jnp.where(below_diag, jnp.exp(ls[d][0] + run[d][:, :t]), 0.0)]
                c_new = run[d][:, t:]
                if len(blocks[a]) == 2:
                    l = (a, 1, h)
                    weights.append(jnp.exp(ls[l][0] + run[l][:, :t] + c_new))
                    c_new = c_new + run[l][:, t:]
                acc = acc + _dot(jnp.concatenate(weights, axis=1).astype(BF16),
                                 jnp.concatenate([vh[a][b][h] for b in blocks[a]], axis=0))
                c_ref[a, h] = c_new
                top = jnp.maximum(top, jnp.max(c_new))
            acc_ref[a] = acc
            tops.append(top)
        for a in range(SB_GROUP):
            sweep(tiles[a], a, jnp.int32(tiles[a] - 2), tops[a])

    tile_group(0)

    def later_group(g, carry):
        tile_group(g * SB_GROUP)
        return carry

    lax.fori_loop(1, seq // (SB_GROUP * t), later_group, 0)


def _sb_attention(proj, n_batch, seq, q_col, k_col, v_col, n_pairs):
    blk = (1, seq, LANES)
    return pl.pallas_call(
        _sb_attn_kernel,
        grid=(n_batch, n_pairs),
        in_specs=[pl.BlockSpec(blk, lambda b, p: (b, 0, q_col + p)),
                  pl.BlockSpec(blk, lambda b, p: (b, 0, k_col + p)),
                  pl.BlockSpec(blk, lambda b, p: (b, 0, v_col + p))],
        out_specs=pl.BlockSpec(blk, lambda b, p: (b, 0, p)),
        out_shape=jax.ShapeDtypeStruct((n_batch, seq, n_pairs * LANES), BF16),
        scratch_shapes=[pltpu.VMEM((SB_GROUP, SB_TILE, LANES), F32),
                        pltpu.VMEM((SB_GROUP, 2, SB_TILE, LANES), F32)],
        compiler_params=_params("parallel", "parallel"),
        name="sb_attention",
    )(proj, proj, proj)


def _pool_mixer_out(u_ref, halo_ref, attn_ref, pw_ref, ps_ref, wo_ref, s_ref, seq):
    tm = u_ref.shape[0]
    t0 = (pl.program_id(0) * tm) % seq
    u = u_ref[...].astype(F32)
    halo = halo_ref[...].astype(F32)
    s_ref[0:POOL_HALO, :] = jnp.where(t0 == 0, jnp.zeros_like(halo), halo)
    s_ref[POOL_HALO:, :] = u
    tpos = t0 + lax.broadcasted_iota(jnp.int32, (tm, 1), 0)
    parts = []
    for g, win in enumerate(POOL_WINDOWS):
        cs = slice(g * LANES, (g + 1) * LANES)
        ug = u[:, cs]
        ws = ug
        for k in range(1, win):
            ws = ws + s_ref[POOL_HALO - k:POOL_HALO - k + tm, cs]
        count = jnp.minimum(tpos + 1, win).astype(F32)
        y = ws / count - ug
        parts.append((_dot(y.astype(BF16), pw_ref[g]) * ps_ref[:, cs]).astype(BF16))
    cat = jnp.concatenate(parts + [attn_ref[...]], axis=1)
    return _dot(cat, wo_ref[...])


def _ffn_ple_tail(x, p_ref, fg_ref, wup_ref, cw_ref, wdn_ref, pg_ref, wg_ref, wp_ref, fn_ref, o_ref,
                  h_ref, ybuf_ref, carry_ref, *, seq, tf, final_norm):
    tm = x.shape[0]
    ffn = wdn_ref.shape[0]
    first = (pl.program_id(0) * tm) % seq == 0
    xn = _rms(x, fg_ref[...]).astype(BF16)
    for c in range(ffn // tf):
        conv = []
        for part in range(2):
            cs = slice(part * ffn + c * tf, part * ffn + (c + 1) * tf)
            y = _dot(xn, wup_ref[:, cs])
            prev = carry_ref[:, cs]
            ybuf_ref[part, 0:CONV_HALO, :] = jnp.where(first, jnp.zeros_like(prev), prev)
            ybuf_ref[part, CONV_HALO:, :] = y
            carry_ref[:, cs] = y[tm - CONV_HALO:, :]
            w = cw_ref[:, cs]
            z = w[FFN_CONV - 1:FFN_CONV] * y
            for i in range(1, FFN_CONV):
                z = z + w[FFN_CONV - 1 - i:FFN_CONV - i] * ybuf_ref[part, CONV_HALO - i:CONV_HALO - i + tm, :]
            conv.append(z)
        h_ref[:, c * tf:(c + 1) * tf] = (conv[0] * jax.nn.sigmoid(conv[0]) * conv[1]).astype(BF16)
    x2 = x + _dot(h_ref[...], wdn_ref[...])
    gate = jax.nn.sigmoid(_dot(_rms(x2, pg_ref[...]).astype(BF16), wg_ref[...]))
    x3 = x2 + _dot(p_ref[...].astype(BF16), wp_ref[...]) * gate
    if final_norm:
        x3 = _rms(x3, fn_ref[...])
    o_ref[...] = x3


def _even_tail_kernel(x_ref, u_ref, halo_ref, attn_ref, pw_ref, ps_ref, wo_ref, *rest, seq, tf, final_norm):
    *ffn_refs, s_ref, h_ref, ybuf_ref, carry_ref = rest
    x = x_ref[...] + _pool_mixer_out(u_ref, halo_ref, attn_ref, pw_ref, ps_ref, wo_ref, s_ref, seq)
    _ffn_ple_tail(x, *ffn_refs, h_ref, ybuf_ref, carry_ref, seq=seq, tf=tf, final_norm=final_norm)


def _odd_tail_kernel(x_ref, a_ref, wo_ref, *rest, seq, tf, final_norm):
    x = x_ref[...] + _dot(a_ref[...], wo_ref[...])
    _ffn_ple_tail(x, *rest, seq=seq, tf=tf, final_norm=final_norm)


def _layer_tail(x, mixer, p, ffn_gain, w_up, conv_w, w_down, ple_gain, w_gate, w_ple, final_gain, seq, final_norm,
                *, tm=ROW_TILE, tf=FFN_CHUNK):
    m, d = x.shape
    row = lambda r: (r, 0)
    layer, p = p
    p_row = lambda r: (layer * (m // tm) + r, 0)
    scratch = [pltpu.VMEM((tm, w_down.shape[0]), BF16), pltpu.VMEM((2, tm + CONV_HALO, tf), F32),
               pltpu.VMEM((CONV_HALO, w_up.shape[1]), F32)]
    if mixer[0] == "even":
        _, proj, attn, pool_w, pool_scale, w_out = mixer
        pw = len(POOL_WINDOWS) * LANES
        hb = tm // POOL_HALO
        body = _even_tail_kernel
        mixer_args = (proj, proj, attn, pool_w, pool_scale, w_out)
        mixer_specs = [pl.BlockSpec((tm, pw), row),
                       pl.BlockSpec((POOL_HALO, pw), lambda r: (jnp.maximum(r * hb - 1, 0), 0)),
                       pl.BlockSpec((tm, attn.shape[1]), row),
                       _const_spec(pool_w.shape), _const_spec(pool_scale.shape), _const_spec(w_out.shape)]
        scratch = [pltpu.VMEM((tm + POOL_HALO, pw), F32)] + scratch
    else:
        _, o, w_out = mixer
        body = _odd_tail_kernel
        mixer_args = (o, w_out)
        mixer_specs = [pl.BlockSpec((tm, o.shape[1]), row), _const_spec(w_out.shape)]
    return pl.pallas_call(
        functools.partial(body, seq=seq, tf=tf, final_norm=final_norm),
        grid=(m // tm,),
        in_specs=[pl.BlockSpec((tm, d), row)] + mixer_specs
                 + [pl.BlockSpec((tm, p.shape[1]), p_row),
                    _const_spec((1, d)), _const_spec(w_up.shape), _const_spec(conv_w.shape),
                    _const_spec(w_down.shape), _const_spec((1, d)), _const_spec(w_gate.shape),
                    _const_spec(w_ple.shape), _const_spec((1, d))],
        out_specs=pl.BlockSpec((tm, d), row),
        out_shape=jax.ShapeDtypeStruct((m, d), F32),
        scratch_shapes=scratch,
        compiler_params=_params("arbitrary"),
        name=mixer[0] + "_tail",
    )(x, *mixer_args, p, ffn_gain, w_up, conv_w, w_down, ple_gain, w_gate, w_ple, final_gain)


def _gdn_ops_kernel(q_ref, k_ref, v_ref, gb_ref, kwq_ref, n_ref, o0_ref, gl_ref, tr_ref, *, n_heads, hg):
    tile, chunk, dh = GDN_TILE, GDN_CHUNK, GDN_HEAD_DIM
    cpt = tile // chunk
    row = lax.broadcasted_iota(jnp.int32, (tile, tile), 0)
    col = lax.broadcasted_iota(jnp.int32, (tile, tile), 1)
    same = (row // chunk) == (col // chunk)
    incl = jnp.logical_and(same, row >= col)
    strict = jnp.logical_and(same, row > col)
    lane = lax.broadcasted_iota(jnp.int32, (tile, LANES), 1)

    def block_diag(wide):
        return jnp.where(same, jnp.concatenate([wide] * cpt, axis=0), 0.0).astype(BF16)

    streams = [(dt, h) for dt in range(GDN_TILES_PER_STEP) for h in range(hg)]
    ns = range(len(streams))
    hls = [slice(h * dh, (h + 1) * dh) for _, h in streams]
    heads = [pl.program_id(1) * hg + h for _, h in streams]

    r0s = [dt * tile for dt, _ in streams]
    gbts = [gb_ref[0, r0:r0 + tile, :] for r0 in r0s]
    for s in ns:
        if streams[s][1] == 0:
            tr_ref[streams[s][0]] = gbts[s].T
    q = [q_ref[0, r0s[s]:r0s[s] + tile, hls[s]].astype(F32) for s in ns]
    k16 = [k_ref[0, r0s[s]:r0s[s] + tile, hls[s]] for s in ns]
    k = [k16[s].astype(F32) for s in ns]
    v = [v_ref[0, r0s[s]:r0s[s] + tile, hls[s]].astype(F32) for s in ns]
    beta = [jnp.sum(jnp.where(lane == heads[s], gbts[s], 0.0), axis=-1, keepdims=True) for s in ns]
    gc = [jnp.sum(jnp.where(lane == n_heads + heads[s], gbts[s], 0.0), axis=-1, keepdims=True) for s in ns]
    gc_b = [jnp.broadcast_to(gc[s], (tile, dh)) for s in ns]
    gc_last = [jnp.broadcast_to(gc_b[s].reshape(cpt, chunk, dh)[:, chunk - 1:chunk, :],
                                (cpt, chunk, dh)).reshape(tile, dh) for s in ns]
    gc_row = [tr_ref[streams[s][0], pl.ds(n_heads + heads[s], 1), :] for s in ns]
    decay = [jnp.where(incl, jnp.exp(jnp.where(incl, gc[s] - gc_row[s], 0.0)), 0.0) for s in ns]
    k_beta = [k[s] * beta[s] for s in ns]
    kq = [_dot_nt(jnp.concatenate([k_beta[s].astype(BF16), q[s].astype(BF16)], axis=0), k16[s]) for s in ns]
    a_bd = [jnp.where(strict, kq[s][:tile] * decay[s], 0.0) for s in ns]
    qk16 = [(kq[s][tile:] * decay[s]).astype(BF16) for s in ns]
    pw = [-(a_bd[s][0:chunk] + a_bd[s][chunk:2 * chunk] + a_bd[s][2 * chunk:3 * chunk] + a_bd[s][3 * chunk:])
          for s in ns]
    res = list(pw)
    pw = [_dot(pw[s].astype(BF16), block_diag(pw[s])) for s in ns]
    step = 2
    while step < chunk:
        both = [_dot(jnp.concatenate([res[s], pw[s]], axis=0).astype(BF16), block_diag(pw[s])) for s in ns]
        res = [res[s] + pw[s] + both[s][:chunk] for s in ns]
        pw = [both[s][chunk:] for s in ns]
        step *= 2
    egc = [jnp.exp(gc_b[s]) for s in ns]
    rhs = [jnp.concatenate([v[s] * beta[s], k_beta[s] * egc[s]], axis=1) for s in ns]
    sol = [rhs[s] + _dot(block_diag(res[s]), rhs[s].astype(BF16)) for s in ns]
    sol16 = [sol[s].astype(BF16) for s in ns]
    qkuw = [_dot(qk16[s], sol16[s]) for s in ns]
    kd16 = [(k[s] * jnp.exp(gc_last[s] - gc_b[s])).astype(BF16) for s in ns]
    for s in ns:
        hl = hls[s]
        o0_ref[0, r0s[s]:r0s[s] + tile, hl] = qkuw[s][:, :dh].astype(o0_ref.dtype)
        qe16 = (q[s] * egc[s] - qkuw[s][:, dh:]).astype(BF16)
        gl = jnp.exp(gc_last[s])
        for c in range(cpt):
            cr = slice(c * chunk, (c + 1) * chunk)
            ci = streams[s][0] * cpt + c
            nkw = _dot_tn(kd16[s][cr], sol16[s][cr])
            n_ref[0, 0, ci, :, hl] = nkw[:, :dh].astype(n_ref.dtype)
            kwq_ref[0, 0, ci, 0:dh, hl] = nkw[:, dh:].astype(BF16)
            kwq_ref[0, 0, ci, dh:, hl] = qe16[cr]
            gl_ref[0, 0, ci, :, hl] = gl[c * chunk:c * chunk + 8]


def _gdn_scan_kernel(kwq_ref, n_ref, o0_ref, gl_ref, z_ref, nw_ref, o_ref, s_ref, *, hg):
    chunk, dh = GDN_CHUNK, GDN_HEAD_DIM
    nb, groups, n_chunks = kwq_ref.shape[:3]
    width = hg * dh
    chains = [(b, p) for b in range(nb) for p in range(groups)]

    @pl.when(pl.program_id(1) == 0)
    def _():
        s_ref[...] = jnp.zeros_like(s_ref)

    zero = jnp.zeros((dh, dh), BF16)

    def scan(n, carry):
        rows = pl.ds(pl.multiple_of(n * chunk, chunk), chunk)
        state = [s_ref[b, p] for b, p in chains]
        s16 = [st.astype(BF16) for st in state]
        s_bd = [jnp.concatenate([jnp.concatenate([x[:, :dh], zero], axis=1),
                                 jnp.concatenate([zero, x[:, dh:]], axis=1)], axis=0) for x in s16]
        both = [_dot(kwq_ref[b, p, n], s_bd[i]) for i, (b, p) in enumerate(chains)]
        for i, (b, p) in enumerate(chains):
            s_ref[b, p] = (state[i] * jnp.concatenate([gl_ref[b, p, n]] * (dh // 8), axis=0)
                           + n_ref[b, p, n].astype(F32) - both[i][:dh])
        for i, (b, p) in enumerate(chains):
            o = both[i][dh:] + o0_ref[b, rows, p * width:(p + 1) * width].astype(F32)
            for h in range(hg):
                hl = slice(p * width + h * dh, p * width + (h + 1) * dh)
                oh = o[:, h * dh:(h + 1) * dh]
                oh = oh * lax.rsqrt(jnp.mean(oh * oh, axis=-1, keepdims=True) + EPS) * nw_ref[...]
                zg = z_ref[b, rows, hl].astype(F32)
                o_ref[b, rows, hl] = (oh * (zg * jax.nn.sigmoid(zg))).astype(o_ref.dtype)
        return carry

    lax.fori_loop(0, n_chunks, scan, 0)


def _gdn(proj, gb, norm_w, n_batch, seq, n_heads):
    hg = 2
    dh, chunk = GDN_HEAD_DIM, GDN_CHUNK
    width = hg * dh
    groups = n_heads // hg
    mix = n_heads * dh
    rows = GDN_TILE * GDN_TILES_PER_STEP
    cps = rows // chunk
    n_chunks = seq // chunk
    col = lambda base: (lambda b, p, t: (b, t, base * groups + p))
    per_chunk = lambda r: pl.BlockSpec((1, 1, cps, r, width), lambda b, p, t: (b, p, t, 0, 0))
    kwq, n_op, o0, gl = pl.pallas_call(
        functools.partial(_gdn_ops_kernel, n_heads=n_heads, hg=hg),
        grid=(n_batch, groups, seq // rows),
        in_specs=[pl.BlockSpec((1, rows, width), col(0)), pl.BlockSpec((1, rows, width), col(1)),
                  pl.BlockSpec((1, rows, width), col(2)),
                  pl.BlockSpec((1, rows, LANES), lambda b, p, t: (b, t, 0))],
        out_specs=[per_chunk(dh + chunk), per_chunk(dh),
                   pl.BlockSpec((1, rows, width), lambda b, p, t: (b, t, p)), per_chunk(8)],
        out_shape=[jax.ShapeDtypeStruct((n_batch, groups, n_chunks, dh + chunk, width), BF16),
                   jax.ShapeDtypeStruct((n_batch, groups, n_chunks, dh, width), BF16),
                   jax.ShapeDtypeStruct((n_batch, seq, mix), BF16),
                   jax.ShapeDtypeStruct((n_batch, groups, n_chunks, 8, width), F32)],
        scratch_shapes=[pltpu.VMEM((GDN_TILES_PER_STEP, LANES, GDN_TILE), F32)],
        compiler_params=_params("parallel", "parallel", "parallel"),
        name="gdn_ops",
    )(proj, proj, proj, gb)
    nb = GDN_SCAN_BATCH if n_batch % GDN_SCAN_BATCH == 0 else 1
    srows = GDN_SCAN_ROWS
    scps = srows // chunk
    all_pairs = lambda r: pl.BlockSpec((nb, groups, scps, r, width), lambda b, t: (b, 0, t, 0, 0))
    return pl.pallas_call(
        functools.partial(_gdn_scan_kernel, hg=hg),
        grid=(n_batch // nb, seq // srows),
        in_specs=[all_pairs(dh + chunk), all_pairs(dh),
                  pl.BlockSpec((nb, srows, mix), lambda b, t: (b, t, 0)), all_pairs(8),
                  pl.BlockSpec((nb, srows, mix), lambda b, t: (b, t, 3)),
                  _const_spec((1, dh))],
        out_specs=pl.BlockSpec((nb, srows, mix), lambda b, t: (b, t, 0)),
        out_shape=jax.ShapeDtypeStruct((n_batch, seq, mix), BF16),
        scratch_shapes=[pltpu.VMEM((nb, groups, dh, width), F32)],
        compiler_params=_params("parallel", "arbitrary"),
        name="gdn_scan",
    )(kwq, n_op, o0, gl, proj, norm_w)


def kernel(x, p, mix_norm_e, w_in_e, pool_w, pool_scale, w_out_e, mix_norm_o, w_in_o, conv_qkv_o, a_log_o,
           dt_bias_o, gdn_norm_o, w_out_o, ffn_norm, w_up, ffn_conv, w_down, ple_norm, w_ple_gate, w_ple,
           final_norm):
    n_batch, seq, d = x.shape
    depth = p.shape[0]
    m = n_batch * seq
    pool_width = len(POOL_WINDOWS) * LANES
    row = lambda a: a.reshape(1, -1)
    xf = x.reshape(m, d)
    for i in range(depth):
        j = i // 2
        if i % 2 == 0:
            sb_width = (w_in_e.shape[2] - pool_width) // 3
            scale = jnp.concatenate([jnp.ones((pool_width,), F32),
                                     jnp.full((sb_width,), SB_HEAD_DIM ** -0.5, F32),
                                     jnp.ones((2 * sb_width,), F32)])
            proj = _norm_matmul(xf, row(mix_norm_e[j]), (w_in_e[j] * scale).astype(BF16))
            pc, sc = pool_width // LANES, sb_width // LANES
            attn = _sb_attention(proj.reshape(n_batch, seq, -1), n_batch, seq, pc, pc + sc, pc + 2 * sc, sc)
            mixer = ("even", proj, attn.reshape(m, sb_width), pool_w[j].astype(BF16), row(pool_scale[j]),
                     w_out_e[j].astype(BF16))
        else:
            n_heads = a_log_o.shape[1]
            mix = n_heads * GDN_HEAD_DIM
            wba = jnp.pad(w_in_o[j][:, 4 * mix:], ((0, 0), (0, LANES - 2 * n_heads))).astype(BF16)
            lane_pad = lambda a: jnp.pad(a, (n_heads, LANES - 2 * n_heads)).reshape(1, LANES)
            proj, gb = _odd_in_proj(xf, row(mix_norm_o[j]), w_in_o[j][:, :4 * mix].astype(BF16), conv_qkv_o[j],
                                    wba, lane_pad(a_log_o[j]), lane_pad(dt_bias_o[j]), n_heads, seq)
            o = _gdn(proj.reshape(n_batch, seq, -1), gb.reshape(n_batch, seq, LANES), row(gdn_norm_o[j]),
                     n_batch, seq, n_heads)
            mixer = ("odd", o.reshape(m, mix), w_out_o[j].astype(BF16))
        xf = _layer_tail(xf, mixer, (i, p.reshape(depth * m, -1)), row(ffn_norm[i]), w_up[i].astype(BF16), ffn_conv[i],
                         w_down[i].astype(BF16), row(ple_norm[i]), w_ple_gate[i].astype(BF16),
                         w_ple[i].astype(BF16), row(final_norm), seq, i == depth - 1)
    return xf.reshape(n_batch, seq, d)
```

```python
import functools

import jax
import jax.numpy as jnp
from jax import lax
from jax.experimental import pallas as pl
from jax.experimental.pallas import tpu as pltpu

F32 = jnp.float32
BF16 = jnp.bfloat16

EPS = 1e-6
LANES = 128
SUBLANES = 8
POOL_WINDOWS = (2, 4, 8, 16)
POOL_HALO = 16
SB_HEAD_DIM = 64
SB_TILE = 128
SB_GROUP = 8
SB_LOG_UNDERFLOW = 88.0
GDN_HEAD_DIM = 128
GDN_CHUNK = 64
GDN_TILE = 256
GDN_TILES_PER_STEP = 8
GDN_SCAN_BATCH = 2
GDN_SCAN_ROWS = 512
GDN_CONV = 4
FFN_CONV = 3
CONV_HALO = SUBLANES
ROW_TILE = 512
ODD_IN_ROW_TILE = 1024
FFN_CHUNK = 256
VMEM_LIMIT_BYTES = 56 * 1024 * 1024


def _params(*sem):
    return pltpu.CompilerParams(dimension_semantics=sem, vmem_limit_bytes=VMEM_LIMIT_BYTES)


def _const_spec(shape):
    nd = len(shape)
    return pl.BlockSpec(shape, lambda *_: (0,) * nd, pipeline_mode=pl.Buffered(1))


def _rms(x, gain):
    return x * lax.rsqrt(jnp.mean(x * x, axis=-1, keepdims=True) + EPS) * gain


def _dot(a, b):
    return jnp.dot(a, b, preferred_element_type=F32)


def _dot_nt(a, b):
    return lax.dot_general(a, b, (((1,), (1,)), ((), ())), preferred_element_type=F32)


def _dot_tn(a, b):
    return lax.dot_general(a, b, (((0,), (0,)), ((), ())), preferred_element_type=F32)


def _split_bf16(x):
    hi = x.astype(BF16)
    lo = (x - hi.astype(F32)).astype(BF16)
    return hi, lo


def _norm_matmul_kernel(x_ref, g_ref, w_ref, o_ref, *, n_chunk):
    xn = _rms(x_ref[...], g_ref[...]).astype(BF16)
    for c in range(0, w_ref.shape[1], n_chunk):
        o_ref[:, c:c + n_chunk] = _dot(xn, w_ref[:, c:c + n_chunk]).astype(o_ref.dtype)


def _norm_matmul(x, gain, w, *, tm=ROW_TILE, n_chunk=512):
    m, d = x.shape
    n = w.shape[1]
    return pl.pallas_call(
        functools.partial(_norm_matmul_kernel, n_chunk=n_chunk),
        grid=(m // tm,),
        in_specs=[pl.BlockSpec((tm, d), lambda r: (r, 0)), _const_spec((1, d)), _const_spec((d, n))],
        out_specs=pl.BlockSpec((tm, n), lambda r: (r, 0)),
        out_shape=jax.ShapeDtypeStruct((m, n), BF16),
        compiler_params=_params("parallel"),
        name="norm_matmul",
    )(x, gain, w)


def _odd_in_kernel(x_ref, g_ref, w_ref, cw_ref, wba_ref, alog_ref, dt_ref, o_ref, gb_ref, xs_ref, perm_ref,
                   carry_ref, *, n_chunk, n_heads, seq):
    tm = x_ref.shape[0]
    mix = n_heads * GDN_HEAD_DIM
    nb = tm // SUBLANES
    hist = (GDN_CONV - 1) * SUBLANES
    first = (pl.program_id(0) * tm) % seq == 0
    sub0 = lax.broadcasted_iota(jnp.int32, (SUBLANES, n_chunk), 0) == 0

    def scattered(a):
        per = nb // SUBLANES
        return pl.ds(SUBLANES * SUBLANES * (a % per) + a // per, SUBLANES, stride=SUBLANES)

    slabs = x_ref.shape[1] // LANES
    for j in range(slabs):
        for a in range(nb):
            xs_ref[j, scattered(a), :] = x_ref[a * SUBLANES:(a + 1) * SUBLANES, j * LANES:(j + 1) * LANES]
    xn = _rms(jnp.concatenate([xs_ref[j] for j in range(slabs)], axis=1), g_ref[...]).astype(BF16)

    def time_order(y):
        width = y.shape[1] // LANES
        for j in range(width):
            perm_ref[j] = y[:, j * LANES:(j + 1) * LANES]
        return jnp.concatenate(
            [jnp.concatenate([perm_ref[j, scattered(a), :] for j in range(width)], axis=1) for a in range(nb)],
            axis=0)

    for c in range(0, w_ref.shape[1], n_chunk):
        cs = slice(c, c + n_chunk)
        y = _dot(xn, w_ref[:, cs])
        if c < 3 * mix:
            prev = carry_ref[:, cs]
            prev = jnp.where(first, jnp.zeros_like(prev), prev)
            tail = y[tm - hist:, :]
            carry_ref[:, cs] = tail
            wrap = jnp.concatenate(
                [jnp.where(sub0, pltpu.roll(prev[b:b + SUBLANES], 1, axis=0),
                           pltpu.roll(tail[b:b + SUBLANES], 1, axis=0)) for b in range(0, hist, SUBLANES)], axis=0)
            w = cw_ref[:, cs]
            z = w[GDN_CONV - 1:GDN_CONV] * y
            for i in range(1, GDN_CONV):
                back = jnp.concatenate([wrap[hist - i * SUBLANES:], y[:tm - i * SUBLANES]], axis=0)
                z = z + w[GDN_CONV - 1 - i:GDN_CONV - i] * back
            y = z * jax.nn.sigmoid(z)
            if c < 2 * mix:
                scale = GDN_HEAD_DIM ** -0.5 if c < mix else 1.0
                heads = []
                for hc in range(0, n_chunk, GDN_HEAD_DIM):
                    yh = y[:, hc:hc + GDN_HEAD_DIM]
                    inv = lax.rsqrt(jnp.sum(yh * yh, axis=-1, keepdims=True) + EPS)
                    heads.append(yh * (inv * scale))
                y = jnp.concatenate(heads, axis=1)
        o_ref[:, cs] = time_order(y).astype(o_ref.dtype)
    ba = time_order(_dot(xn, wba_ref[...]))
    lane = lax.broadcasted_iota(jnp.int32, ba.shape, 1)
    is_g = jnp.logical_and(lane >= n_heads, lane < 2 * n_heads)
    g = jnp.where(is_g, -jnp.exp(alog_ref[...]) * jax.nn.softplus(ba + dt_ref[...]), 0.0)
    beta = jax.nn.sigmoid(ba)
    sub = GDN_TILE
    row = lax.broadcasted_iota(jnp.int32, (sub, sub), 0)
    col = lax.broadcasted_iota(jnp.int32, (sub, sub), 1)
    tri = jnp.where(jnp.logical_and(row // GDN_CHUNK == col // GDN_CHUNK, row >= col), 1.0, 0.0).astype(BF16)
    tri2 = jnp.concatenate([tri, tri], axis=1)
    is_beta = lax.broadcasted_iota(jnp.int32, (sub, LANES), 1) < n_heads
    for r in range(0, ba.shape[0], sub):
        hi, lo = _split_bf16(g[r:r + sub])
        gc = _dot(tri2, jnp.concatenate([hi, lo], axis=0))
        gb_ref[r:r + sub, :] = jnp.where(is_beta, beta[r:r + sub], gc)


def _odd_in_proj(x, gain, w, conv_w, wba, alog_lane, dt_lane, n_heads, seq, *, tm=ODD_IN_ROW_TILE, n_chunk=512):
    m, d = x.shape
    n = w.shape[1]
    return pl.pallas_call(
        functools.partial(_odd_in_kernel, n_chunk=n_chunk, n_heads=n_heads, seq=seq),
        grid=(m // tm,),
        in_specs=[pl.BlockSpec((tm, d), lambda r: (r, 0)), _const_spec((1, d)), _const_spec((d, n)),
                  _const_spec(conv_w.shape), _const_spec((d, LANES)), _const_spec((1, LANES)),
                  _const_spec((1, LANES))],
        out_specs=[pl.BlockSpec((tm, n), lambda r: (r, 0)), pl.BlockSpec((tm, LANES), lambda r: (r, 0))],
        out_shape=[jax.ShapeDtypeStruct((m, n), BF16), jax.ShapeDtypeStruct((m, LANES), F32)],
        scratch_shapes=[pltpu.VMEM((d // LANES, tm, LANES), F32), pltpu.VMEM((n_chunk // LANES, tm, LANES), F32),
                        pltpu.VMEM(((GDN_CONV - 1) * SUBLANES, conv_w.shape[1]), F32)],
        compiler_params=_params("arbitrary"),
        name="odd_in_proj",
    )(x, gain, w, conv_w, wba, alog_lane, dt_lane)


def _sb_attn_kernel(q_ref, k_ref, v_ref, o_ref, acc_ref, c_ref):
    t = SB_TILE
    seq = q_ref.shape[1]
    lane = lax.broadcasted_iota(jnp.int32, (t, LANES), 1)
    head0 = lane < SB_HEAD_DIM
    row = lax.broadcasted_iota(jnp.int32, (t, t), 0)
    col = lax.broadcasted_iota(jnp.int32, (t, t), 1)
    r2 = lax.broadcasted_iota(jnp.int32, (2 * t, 2 * t), 0) % t
    c2 = lax.broadcasted_iota(jnp.int32, (2 * t, 2 * t), 1)
    cum = jnp.where((c2 >= t) | (r2 > c2), 1.0, 0.0).astype(BF16)

    below_diag = col < row

    def rows_of(i):
        return pl.ds(i * t, t) if isinstance(i, int) else pl.ds(pl.multiple_of(i * t, t), t)

    def split_heads(x):
        zero = jnp.zeros_like(x)
        return (jnp.where(head0, x, zero), jnp.where(head0, zero, x))

    def log_sigmoids(qm, kb):
        z = _dot_nt(qm, kb)
        ls_pos = jnp.minimum(z, 0.0) - jnp.log(1.0 + jnp.exp(-jnp.abs(z)))
        return ls_pos, ls_pos - z

    def running(ls_neg):
        hi, lo = _split_bf16(ls_neg)
        return _dot(jnp.concatenate([hi, lo], axis=1), cum)

    def sweep(i, slot, j_start, top_start):
        qh = split_heads(q_ref[0, rows_of(i), :])

        def cond(st):
            j, top = st
            return jnp.logical_and(j >= 0, top > -SB_LOG_UNDERFLOW)

        def body(st):
            j, _ = st
            kb = k_ref[0, rows_of(j), :]
            vh = split_heads(v_ref[0, rows_of(j), :])
            valid = (col - row) < (i - j) * t
            acc = acc_ref[slot]
            top = jnp.float32(-jnp.inf)
            for h in range(2):
                ls_pos, ls_neg = log_sigmoids(qh[h], kb)
                run = running(jnp.where(valid, ls_neg, 0.0))
                c_old = c_ref[slot, h]
                a = jnp.where(valid, jnp.exp(ls_pos + run[:, :t] + c_old), 0.0)
                acc = acc + _dot(a.astype(BF16), vh[h])
                c_new = c_old + run[:, t:]
                c_ref[slot, h] = c_new
                top = jnp.maximum(top, jnp.max(c_new))
            acc_ref[slot] = acc
            return j - 1, top

        lax.while_loop(cond, body, (j_start, top_start))
        o_ref[0, rows_of(i), :] = acc_ref[slot].astype(o_ref.dtype)

    def tile_group(first):
        tiles = [first + a for a in range(SB_GROUP)]
        blocks = [[0] if isinstance(i, int) and i == 0 else [0, 1] for i in tiles]
        streams = [(a, b, h) for a in range(SB_GROUP) for b in blocks[a] for h in range(2)]
        qh = [split_heads(q_ref[0, rows_of(i), :]) for i in tiles]
        kb = [[k_ref[0, rows_of(i - b), :] for b in blocks[a]] for a, i in enumerate(tiles)]
        vh = [[split_heads(v_ref[0, rows_of(i - b), :]) for b in blocks[a]] for a, i in enumerate(tiles)]
        ls = {s: log_sigmoids(qh[s[0]][s[2]], kb[s[0]][s[1]]) for s in streams}
        run = {s: running(jnp.where(below_diag, ls[s][1], 0.0) if s[1] == 0 else ls[s][1]) for s in streams}
        tops = []
        for a in range(SB_GROUP):
            acc = jnp.zeros((t, LANES), F32)
            top = jnp.float32(-jnp.inf)
            for h in range(2):
                d = (a, 0, h)
                weights = [jnp.where(below_diag, jnp.exp(ls[d][0] + run[d][:, :t]), 0.0)]
                c_new = run[d][:, t:]
                if len(blocks[a]) == 2:
                    l = (a, 1, h)
                    weights.append(jnp.exp(ls[l][0] + run[l][:, :t] + c_new))
                    c_new = c_new + run[l][:, t:]
                acc = acc + _dot(jnp.concatenate(weights, axis=1).astype(BF16),
                                 jnp.concatenate([vh[a][b][h] for b in blocks[a]], axis=0))
                c_ref[a, h] = c_new
                top = jnp.maximum(top, jnp.max(c_new))
            acc_ref[a] = acc
            tops.append(top)
        for a in range(SB_GROUP):
            sweep(tiles[a], a, jnp.int32(tiles[a] - 2), tops[a])

    tile_group(0)

    def later_group(g, carry):
        tile_group(g * SB_GROUP)
        return carry

    lax.fori_loop(1, seq // (SB_GROUP * t), later_group, 0)


def _sb_attention(proj, n_batch, seq, q_col, k_col, v_col, n_pairs):
    blk = (1, seq, LANES)
    return pl.pallas_call(
        _sb_attn_kernel,
        grid=(n_batch, n_pairs),
        in_specs=[pl.BlockSpec(blk, lambda b, p: (b, 0, q_col + p)),
                  pl.BlockSpec(blk, lambda b, p: (b, 0, k_col + p)),
                  pl.BlockSpec(blk, lambda b, p: (b, 0, v_col + p))],
        out_specs=pl.BlockSpec(blk, lambda b, p: (b, 0, p)),
        out_shape=jax.ShapeDtypeStruct((n_batch, seq, n_pairs * LANES), BF16),
        scratch_shapes=[pltpu.VMEM((SB_GROUP, SB_TILE, LANES), F32),
                        pltpu.VMEM((SB_GROUP, 2, SB_TILE, LANES), F32)],
        compiler_params=_params("parallel", "parallel"),
        name="sb_attention",
    )(proj, proj, proj)


def _pool_mixer_out(u_ref, halo_ref, attn_ref, pw_ref, ps_ref, wo_ref, s_ref, seq):
    tm = u_ref.shape[0]
    t0 = (pl.program_id(0) * tm) % seq
    u = u_ref[...].astype(F32)
    halo = halo_ref[...].astype(F32)
    s_ref[0:POOL_HALO, :] = jnp.where(t0 == 0, jnp.zeros_like(halo), halo)
    s_ref[POOL_HALO:, :] = u
    tpos = t0 + lax.broadcasted_iota(jnp.int32, (tm, 1), 0)
    parts = []
    for g, win in enumerate(POOL_WINDOWS):
        cs = slice(g * LANES, (g + 1) * LANES)
        ug = u[:, cs]
        ws = ug
        for k in range(1, win):
            ws = ws + s_ref[POOL_HALO - k:POOL_HALO - k + tm, cs]
        count = jnp.minimum(tpos + 1, win).astype(F32)
        y = ws / count - ug
        parts.append((_dot(y.astype(BF16), pw_ref[g]) * ps_ref[:, cs]).astype(BF16))
    cat = jnp.concatenate(parts + [attn_ref[...]], axis=1)
    return _dot(cat, wo_ref[...])


def _ffn_ple_tail(x, p_ref, fg_ref, wup_ref, cw_ref, wdn_ref, pg_ref, wg_ref, wp_ref, fn_ref, o_ref,
                  h_ref, ybuf_ref, carry_ref, *, seq, tf, final_norm):
    tm = x.shape[0]
    ffn = wdn_ref.shape[0]
    first = (pl.program_id(0) * tm) % seq == 0
    xn = _rms(x, fg_ref[...]).astype(BF16)
    for c in range(ffn // tf):
        conv = []
        for part in range(2):
            cs = slice(part * ffn + c * tf, part * ffn + (c + 1) * tf)
            y = _dot(xn, wup_ref[:, cs])
            prev = carry_ref[:, cs]
            ybuf_ref[part, 0:CONV_HALO, :] = jnp.where(first, jnp.zeros_like(prev), prev)
            ybuf_ref[part, CONV_HALO:, :] = y
            carry_ref[:, cs] = y[tm - CONV_HALO:, :]
            w = cw_ref[:, cs]
            z = w[FFN_CONV - 1:FFN_CONV] * y
            for i in range(1, FFN_CONV):
                z = z + w[FFN_CONV - 1 - i:FFN_CONV - i] * ybuf_ref[part, CONV_HALO - i:CONV_HALO - i + tm, :]
            conv.append(z)
        h_ref[:, c * tf:(c + 1) * tf] = (conv[0] * jax.nn.sigmoid(conv[0]) * conv[1]).astype(BF16)
    x2 = x + _dot(h_ref[...], wdn_ref[...])
    gate = jax.nn.sigmoid(_dot(_rms(x2, pg_ref[...]).astype(BF16), wg_ref[...]))
    x3 = x2 + _dot(p_ref[...].astype(BF16), wp_ref[...]) * gate
    if final_norm:
        x3 = _rms(x3, fn_ref[...])
    o_ref[...] = x3


def _even_tail_kernel(x_ref, u_ref, halo_ref, attn_ref, pw_ref, ps_ref, wo_ref, *rest, seq, tf, final_norm):
    *ffn_refs, s_ref, h_ref, ybuf_ref, carry_ref = rest
    x = x_ref[...] + _pool_mixer_out(u_ref, halo_ref, attn_ref, pw_ref, ps_ref, wo_ref, s_ref, seq)
    _ffn_ple_tail(x, *ffn_refs, h_ref, ybuf_ref, carry_ref, seq=seq, tf=tf, final_norm=final_norm)


def _odd_tail_kernel(x_ref, a_ref, wo_ref, *rest, seq, tf, final_norm):
    x = x_ref[...] + _dot(a_ref[...], wo_ref[...])
    _ffn_ple_tail(x, *rest, seq=seq, tf=tf, final_norm=final_norm)


def _layer_tail(x, mixer, p, ffn_gain, w_up, conv_w, w_down, ple_gain, w_gate, w_ple, final_gain, seq, final_norm,
                *, tm=ROW_TILE, tf=FFN_CHUNK):
    m, d = x.shape
    row = lambda r: (r, 0)
    layer, p = p
    p_row = lambda r: (layer * (m // tm) + r, 0)
    scratch = [pltpu.VMEM((tm, w_down.shape[0]), BF16), pltpu.VMEM((2, tm + CONV_HALO, tf), F32),
               pltpu.VMEM((CONV_HALO, w_up.shape[1]), F32)]
    if mixer[0] == "even":
        _, proj, attn, pool_w, pool_scale, w_out = mixer
        pw = len(POOL_WINDOWS) * LANES
        hb = tm // POOL_HALO
        body = _even_tail_kernel
        mixer_args = (proj, proj, attn, pool_w, pool_scale, w_out)
        mixer_specs = [pl.BlockSpec((tm, pw), row),
                       pl.BlockSpec((POOL_HALO, pw), lambda r: (jnp.maximum(r * hb - 1, 0), 0)),
                       pl.BlockSpec((tm, attn.shape[1]), row),
                       _const_spec(pool_w.shape), _const_spec(pool_scale.shape), _const_spec(w_out.shape)]
        scratch = [pltpu.VMEM((tm + POOL_HALO, pw), F32)] + scratch
    else:
        _, o, w_out = mixer
        body = _odd_tail_kernel
        mixer_args = (o, w_out)
        mixer_specs = [pl.BlockSpec((tm, o.shape[1]), row), _const_spec(w_out.shape)]
    return pl.pallas_call(
        functools.partial(body, seq=seq, tf=tf, final_norm=final_norm),
        grid=(m // tm,),
        in_specs=[pl.BlockSpec((tm, d), row)] + mixer_specs
                 + [pl.BlockSpec((tm, p.shape[1]), p_row),
                    _const_spec((1, d)), _const_spec(w_up.shape), _const_spec(conv_w.shape),
                    _const_spec(w_down.shape), _const_spec((1, d)), _const_spec(w_gate.shape),
                    _const_spec(w_ple.shape), _const_spec((1, d))],
        out_specs=pl.BlockSpec((tm, d), row),
        out_shape=jax.ShapeDtypeStruct((m, d), F32),
        scratch_shapes=scratch,
        compiler_params=_params("arbitrary"),
        name=mixer[0] + "_tail",
    )(x, *mixer_args, p, ffn_gain, w_up, conv_w, w_down, ple_gain, w_gate, w_ple, final_gain)


def _gdn_ops_kernel(q_ref, k_ref, v_ref, gb_ref, kwq_ref, n_ref, o0_ref, gl_ref, tr_ref, *, n_heads, hg):
    tile, chunk, dh = GDN_TILE, GDN_CHUNK, GDN_HEAD_DIM
    cpt = tile // chunk
    row = lax.broadcasted_iota(jnp.int32, (tile, tile), 0)
    col = lax.broadcasted_iota(jnp.int32, (tile, tile), 1)
    same = (row // chunk) == (col // chunk)
    incl = jnp.logical_and(same, row >= col)
    strict = jnp.logical_and(same, row > col)
    lane = lax.broadcasted_iota(jnp.int32, (tile, LANES), 1)

    def block_diag(wide):
        return jnp.where(same, jnp.concatenate([wide] * cpt, axis=0), 0.0).astype(BF16)

    streams = [(dt, h) for dt in range(GDN_TILES_PER_STEP) for h in range(hg)]
    ns = range(len(streams))
    hls = [slice(h * dh, (h + 1) * dh) for _, h in streams]
    heads = [pl.program_id(1) * hg + h for _, h in streams]

    r0s = [dt * tile for dt, _ in streams]
    gbts = [gb_ref[0, r0:r0 + tile, :] for r0 in r0s]
    for s in ns:
        if streams[s][1] == 0:
            tr_ref[streams[s][0]] = gbts[s].T
    q = [q_ref[0, r0s[s]:r0s[s] + tile, hls[s]].astype(F32) for s in ns]
    k16 = [k_ref[0, r0s[s]:r0s[s] + tile, hls[s]] for s in ns]
    k = [k16[s].astype(F32) for s in ns]
    v = [v_ref[0, r0s[s]:r0s[s] + tile, hls[s]].astype(F32) for s in ns]
    beta = [jnp.sum(jnp.where(lane == heads[s], gbts[s], 0.0), axis=-1, keepdims=True) for s in ns]
    gc = [jnp.sum(jnp.where(lane == n_heads + heads[s], gbts[s], 0.0), axis=-1, keepdims=True) for s in ns]
    gc_b = [jnp.broadcast_to(gc[s], (tile, dh)) for s in ns]
    gc_last = [jnp.broadcast_to(gc_b[s].reshape(cpt, chunk, dh)[:, chunk - 1:chunk, :],
                                (cpt, chunk, dh)).reshape(tile, dh) for s in ns]
    gc_row = [tr_ref[streams[s][0], pl.ds(n_heads + heads[s], 1), :] for s in ns]
    decay = [jnp.where(incl, jnp.exp(jnp.where(incl, gc[s] - gc_row[s], 0.0)), 0.0) for s in ns]
    k_beta = [k[s] * beta[s] for s in ns]
    kq = [_dot_nt(jnp.concatenate([k_beta[s].astype(BF16), q[s].astype(BF16)], axis=0), k16[s]) for s in ns]
    a_bd = [jnp.where(strict, kq[s][:tile] * decay[s], 0.0) for s in ns]
    qk16 = [(kq[s][tile:] * decay[s]).astype(BF16) for s in ns]
    pw = [-(a_bd[s][0:chunk] + a_bd[s][chunk:2 * chunk] + a_bd[s][2 * chunk:3 * chunk] + a_bd[s][3 * chunk:])
          for s in ns]
    res = list(pw)
    pw = [_dot(pw[s].astype(BF16), block_diag(pw[s])) for s in ns]
    step = 2
    while step < chunk:
        both = [_dot(jnp.concatenate([res[s], pw[s]], axis=0).astype(BF16), block_diag(pw[s])) for s in ns]
        res = [res[s] + pw[s] + both[s][:chunk] for s in ns]
        pw = [both[s][chunk:] for s in ns]
        step *= 2
    egc = [jnp.exp(gc_b[s]) for s in ns]
    rhs = [jnp.concatenate([v[s] * beta[s], k_beta[s] * egc[s]], axis=1) for s in ns]
    sol = [rhs[s] + _dot(block_diag(res[s]), rhs[s].astype(BF16)) for s in ns]
    sol16 = [sol[s].astype(BF16) for s in ns]
    qkuw = [_dot(qk16[s], sol16[s]) for s in ns]
    kd16 = [(k[s] * jnp.exp(gc_last[s] - gc_b[s])).astype(BF16) for s in ns]
    for s in ns:
        hl = hls[s]
        o0_ref[0, r0s[s]:r0s[s] + tile, hl] = qkuw[s][:, :dh].astype(o0_ref.dtype)
        qe16 = (q[s] * egc[s] - qkuw[s][:, dh:]).astype(BF16)
        gl = jnp.exp(gc_last[s])
        for c in range(cpt):
            cr = slice(c * chunk, (c + 1) * chunk)
            ci = streams[s][0] * cpt + c
            nkw = _dot_tn(kd16[s][cr], sol16[s][cr])
            n_ref[0, 0, ci, :, hl] = nkw[:, :dh].astype(n_ref.dtype)
            kwq_ref[0, 0, ci, 0:dh, hl] = nkw[:, dh:].astype(BF16)
            kwq_ref[0, 0, ci, dh:, hl] = qe16[cr]
            gl_ref[0, 0, ci, :, hl] = gl[c * chunk:c * chunk + 8]


def _gdn_scan_kernel(kwq_ref, n_ref, o0_ref, gl_ref, z_ref, nw_ref, o_ref, s_ref, *, hg):
    chunk, dh = GDN_CHUNK, GDN_HEAD_DIM
    nb, groups, n_chunks = kwq_ref.shape[:3]
    width = hg * dh
    chains = [(b, p) for b in range(nb) for p in range(groups)]

    @pl.when(pl.program_id(1) == 0)
    def _():
        s_ref[...] = jnp.zeros_like(s_ref)

    zero = jnp.zeros((dh, dh), BF16)

    def scan(n, carry):
        rows = pl.ds(pl.multiple_of(n * chunk, chunk), chunk)
        state = [s_ref[b, p] for b, p in chains]
        s16 = [st.astype(BF16) for st in state]
        s_bd = [jnp.concatenate([jnp.concatenate([x[:, :dh], zero], axis=1),
                                 jnp.concatenate([zero, x[:, dh:]], axis=1)], axis=0) for x in s16]
        both = [_dot(kwq_ref[b, p, n], s_bd[i]) for i, (b, p) in enumerate(chains)]
        for i, (b, p) in enumerate(chains):
            s_ref[b, p] = (state[i] * jnp.concatenate([gl_ref[b, p, n]] * (dh // 8), axis=0)
                           + n_ref[b, p, n].astype(F32) - both[i][:dh])
        for i, (b, p) in enumerate(chains):
            o = both[i][dh:] + o0_ref[b, rows, p * width:(p + 1) * width].astype(F32)
            for h in range(hg):
                hl = slice(p * width + h * dh, p * width + (h + 1) * dh)
                oh = o[:, h * dh:(h + 1) * dh]
                oh = oh * lax.rsqrt(jnp.mean(oh * oh, axis=-1, keepdims=True) + EPS) * nw_ref[...]
                zg = z_ref[b, rows, hl].astype(F32)
                o_ref[b, rows, hl] = (oh * (zg * jax.nn.sigmoid(zg))).astype(o_ref.dtype)
        return carry

    lax.fori_loop(0, n_chunks, scan, 0)


def _gdn(proj, gb, norm_w, n_batch, seq, n_heads):
    hg = 2
    dh, chunk = GDN_HEAD_DIM, GDN_CHUNK
    width = hg * dh
    groups = n_heads // hg
    mix = n_heads * dh
    rows = GDN_TILE * GDN_TILES_PER_STEP
    cps = rows // chunk
    n_chunks = seq // chunk
    col = lambda base: (lambda b, p, t: (b, t, base * groups + p))
    per_chunk = lambda r: pl.BlockSpec((1, 1, cps, r, width), lambda b, p, t: (b, p, t, 0, 0))
    kwq, n_op, o0, gl = pl.pallas_call(
        functools.partial(_gdn_ops_kernel, n_heads=n_heads, hg=hg),
        grid=(n_batch, groups, seq // rows),
        in_specs=[pl.BlockSpec((1, rows, width), col(0)), pl.BlockSpec((1, rows, width), col(1)),
                  pl.BlockSpec((1, rows, width), col(2)),
                  pl.BlockSpec((1, rows, LANES), lambda b, p, t: (b, t, 0))],
        out_specs=[per_chunk(dh + chunk), per_chunk(dh),
                   pl.BlockSpec((1, rows, width), lambda b, p, t: (b, t, p)), per_chunk(8)],
        out_shape=[jax.ShapeDtypeStruct((n_batch, groups, n_chunks, dh + chunk, width), BF16),
                   jax.ShapeDtypeStruct((n_batch, groups, n_chunks, dh, width), BF16),
                   jax.ShapeDtypeStruct((n_batch, seq, mix), BF16),
                   jax.ShapeDtypeStruct((n_batch, groups, n_chunks, 8, width), F32)],
        scratch_shapes=[pltpu.VMEM((GDN_TILES_PER_STEP, LANES, GDN_TILE), F32)],
        compiler_params=_params("parallel", "parallel", "parallel"),
        name="gdn_ops",
    )(proj, proj, proj, gb)
    nb = GDN_SCAN_BATCH if n_batch % GDN_SCAN_BATCH == 0 else 1
    srows = GDN_SCAN_ROWS
    scps = srows // chunk
    all_pairs = lambda r: pl.BlockSpec((nb, groups, scps, r, width), lambda b, t: (b, 0, t, 0, 0))
    return pl.pallas_call(
        functools.partial(_gdn_scan_kernel, hg=hg),
        grid=(n_batch // nb, seq // srows),
        in_specs=[all_pairs(dh + chunk), all_pairs(dh),
                  pl.BlockSpec((nb, srows, mix), lambda b, t: (b, t, 0)), all_pairs(8),
                  pl.BlockSpec((nb, srows, mix), lambda b, t: (b, t, 3)),
                  _const_spec((1, dh))],
        out_specs=pl.BlockSpec((nb, srows, mix), lambda b, t: (b, t, 0)),
        out_shape=jax.ShapeDtypeStruct((n_batch, seq, mix), BF16),
        scratch_shapes=[pltpu.VMEM((nb, groups, dh, width), F32)],
        compiler_params=_params("parallel", "arbitrary"),
        name="gdn_scan",
    )(kwq, n_op, o0, gl, proj, norm_w)


def kernel(x, p, mix_norm_e, w_in_e, pool_w, pool_scale, w_out_e, mix_norm_o, w_in_o, conv_qkv_o, a_log_o,
           dt_bias_o, gdn_norm_o, w_out_o, ffn_norm, w_up, ffn_conv, w_down, ple_norm, w_ple_gate, w_ple,
           final_norm):
    n_batch, seq, d = x.shape
    depth = p.shape[0]
    m = n_batch * seq
    pool_width = len(POOL_WINDOWS) * LANES
    row = lambda a: a.reshape(1, -1)
    xf = x.reshape(m, d)
    for i in range(depth):
        j = i // 2
        if i % 2 == 0:
            sb_width = (w_in_e.shape[2] - pool_width) // 3
            scale = jnp.concatenate([jnp.ones((pool_width,), F32),
                                     jnp.full((sb_width,), SB_HEAD_DIM ** -0.5, F32),
                                     jnp.ones((2 * sb_width,), F32)])
            proj = _norm_matmul(xf, row(mix_norm_e[j]), (w_in_e[j] * scale).astype(BF16))
            pc, sc = pool_width // LANES, sb_width // LANES
            attn = _sb_attention(proj.reshape(n_batch, seq, -1), n_batch, seq, pc, pc + sc, pc + 2 * sc, sc)
            mixer = ("even", proj, attn.reshape(m, sb_width), pool_w[j].astype(BF16), row(pool_scale[j]),
                     w_out_e[j].astype(BF16))
        else:
            n_heads = a_log_o.shape[1]
            mix = n_heads * GDN_HEAD_DIM
            wba = jnp.pad(w_in_o[j][:, 4 * mix:], ((0, 0), (0, LANES - 2 * n_heads))).astype(BF16)
            lane_pad = lambda a: jnp.pad(a, (n_heads, LANES - 2 * n_heads)).reshape(1, LANES)
            proj, gb = _odd_in_proj(xf, row(mix_norm_o[j]), w_in_o[j][:, :4 * mix].astype(BF16), conv_qkv_o[j],
                                    wba, lane_pad(a_log_o[j]), lane_pad(dt_bias_o[j]), n_heads, seq)
            o = _gdn(proj.reshape(n_batch, seq, -1), gb.reshape(n_batch, seq, LANES), row(gdn_norm_o[j]),
                     n_batch, seq, n_heads)
            mixer = ("odd", o.reshape(m, mix), w_out_o[j].astype(BF16))
        xf = _layer_tail(xf, mixer, (i, p.reshape(depth * m, -1)), row(ffn_norm[i]), w_up[i].astype(BF16), ffn_conv[i],
                         w_down[i].astype(BF16), row(ple_norm[i]), w_ple_gate[i].astype(BF16),
                         w_ple[i].astype(BF16), row(final_norm), seq, i == depth - 1)
    return xf.reshape(n_batch, seq, d)
```
